```python
import jax, jax.numpy as jnp
from jax import lax
import numpy as np

D_MODEL = 4096
BATCH = 4
SEQ = 2048
DEPTH = 4
DEC_BATCH = 8
DEC_SEQ = 8
PAST_LEN = 8192
PAGE_SIZE = 128

N_META = 16
D_FF = 4 * D_MODEL
N_GLA_LAYERS = (DEPTH + 1) // 2
N_SB_LAYERS = DEPTH // 2
GLA_DK = 128
GLA_DV = 256
GLA_HEADS = D_MODEL // 512
GLA_KEY = GLA_HEADS * GLA_DK
GLA_VAL = GLA_HEADS * GLA_DV
GLA_RANK = 16
GLA_TAU = 16.0
GLA_CHUNK = 16
CONV_CH = D_MODEL // 2
CONV_W = 31
SB_HEAD_DIM = 128
SB_HEADS = D_MODEL // SB_HEAD_DIM
SB_WIDTH = SB_HEADS * SB_HEAD_DIM
SB_BLOCK = 128
SB_BIAS_INIT = -6.0
IN_EVEN = 2 * GLA_KEY + 2 * GLA_VAL + GLA_RANK + 2 * CONV_CH
MIX_EVEN = GLA_VAL + CONV_CH
SPLIT_EVEN = (GLA_KEY, 2 * GLA_KEY, 2 * GLA_KEY + GLA_VAL, 2 * GLA_KEY + 2 * GLA_VAL,
              2 * GLA_KEY + 2 * GLA_VAL + GLA_RANK, 2 * GLA_KEY + 2 * GLA_VAL + GLA_RANK + CONV_CH)
EPS = 1e-6

kernel_name = 'hybrid_gla_conformer_stickbreak_decode_step'


def rmsnorm(x, g):
    x32 = x.astype(jnp.float32)
    y = x32 * lax.rsqrt(jnp.mean(x32 * x32, axis=-1, keepdims=True) + EPS)
    return (y * g.astype(jnp.float32)).astype(x.dtype)


def layernorm(x, g, b):
    x32 = x.astype(jnp.float32)
    mu = jnp.mean(x32, axis=-1, keepdims=True)
    xc = x32 - mu
    var = jnp.mean(xc * xc, axis=-1, keepdims=True)
    return (xc * lax.rsqrt(var + EPS) * g.astype(jnp.float32) + b.astype(jnp.float32)).astype(x.dtype)


def gla_recurrence(q, k, v, log_a, s0):
    B, T, H, _ = q.shape
    pad = (-T) % GLA_CHUNK
    n = (T + pad) // GLA_CHUNK

    def blocks(a):
        a = jnp.pad(a.astype(jnp.float32), ((0, 0), (0, pad), (0, 0), (0, 0)))
        return a.reshape(B, n, GLA_CHUNK, H, a.shape[-1]).transpose(1, 0, 2, 3, 4)

    causal = jnp.tril(jnp.ones((GLA_CHUNK, GLA_CHUNK), dtype=bool))

    def step(S, inp):
        qc, kc, vc, gc = inp
        b = jnp.cumsum(gc, axis=1)
        o_inter = jnp.einsum('bthk,bhkv->bthv', qc * jnp.exp(b), S)
        rel = jnp.where(causal[None, :, :, None, None], b[:, :, None] - b[:, None, :], -jnp.inf)
        att = jnp.einsum('bthk,btshk,bshk->bhts', qc, jnp.exp(rel), kc)
        o_intra = jnp.einsum('bhts,bshv->bthv', att, vc)
        b_last = b[:, -1]
        S_new = jnp.exp(b_last)[..., None] * S + jnp.einsum(
            'bshk,bshv->bhkv', kc * jnp.exp(b_last[:, None] - b), vc)
        return S_new, o_inter + o_intra

    S, o = lax.scan(step, s0.astype(jnp.float32), (blocks(q), blocks(k), blocks(v), blocks(log_a)))
    o = o.transpose(1, 0, 2, 3, 4).reshape(B, n * GLA_CHUNK, H, -1)[:, :T]
    return o, S


def causal_depthwise_conv(u_ext, w, b):
    out = lax.conv_general_dilated(u_ext, w[:, None, :].astype(u_ext.dtype), window_strides=(1,),
                                   padding='VALID', dimension_numbers=('NWC', 'WIO', 'NWC'),
                                   feature_group_count=u_ext.shape[-1])
    return out + b.astype(u_ext.dtype)


def stick_breaking_attention(q, k, v, bias, q_offset):
    Tq = q.shape[1]
    scale = SB_HEAD_DIM ** -0.5
    bias32 = bias.astype(jnp.float32)[None, :, None, None]
    outs = []
    for qs in range(0, Tq, SB_BLOCK):
        qe = min(qs + SB_BLOCK, Tq)
        kl = q_offset + qe - 1
        qb = q[:, qs:qe].astype(jnp.float32)
        kb = k[:, :kl].astype(jnp.float32)
        vb = v[:, :kl].astype(jnp.float32)
        z = jnp.einsum('bqhd,bkhd->bhqk', qb, kb) * scale + bias32
        qpos = q_offset + jnp.arange(qs, qe)
        kpos = jnp.arange(kl)
        mask = kpos[None, :] < qpos[:, None]
        sp = jnp.where(mask, jax.nn.softplus(z), 0.0)
        log_w = jax.nn.log_sigmoid(z) - (lax.cumsum(sp, axis=3, reverse=True) - sp)
        w = jnp.where(mask, jnp.exp(log_w), 0.0)
        outs.append(jnp.einsum('bhqk,bkhd->bqhd', w, vb))
    return jnp.concatenate(outs, axis=1).astype(q.dtype)


def even_mixer(h, s0, conv_buf, w_in, w_gate_up, b_gate, gla_g, conv_w, conv_b, ln_g, ln_b, w_out):
    B, T, _ = h.shape
    q, k, v, r, g_lr, glu_a, glu_g = jnp.split(h @ w_in, SPLIT_EVEN, axis=-1)
    q = q.reshape(B, T, GLA_HEADS, GLA_DK) * (GLA_DK ** -0.5)
    k = k.reshape(B, T, GLA_HEADS, GLA_DK)
    v = v.reshape(B, T, GLA_HEADS, GLA_DV)
    log_a = jax.nn.log_sigmoid((g_lr @ w_gate_up + b_gate).astype(jnp.float32)) / GLA_TAU
    log_a = log_a.reshape(B, T, GLA_HEADS, GLA_DK)
    o, s_new = gla_recurrence(q, k, v, log_a, s0)
    o = rmsnorm(o, gla_g).reshape(B, T, GLA_VAL)
    o = (o * jax.nn.silu(r.astype(jnp.float32))).astype(h.dtype)
    u = glu_a * jax.nn.sigmoid(glu_g)
    u_ext = jnp.concatenate([conv_buf.astype(u.dtype), u], axis=1)
    c = jax.nn.silu(layernorm(causal_depthwise_conv(u_ext, conv_w, conv_b), ln_g, ln_b))
    y = jnp.concatenate([o, c.astype(h.dtype)], axis=-1) @ w_out
    return y, s_new.astype(h.dtype), u_ext[:, -(CONV_W - 1):]


def odd_mixer(h, k_past, v_past, w_qkv, sb_bias, w_out):
    B, T, _ = h.shape
    q, k, v = jnp.split(h @ w_qkv, 3, axis=-1)
    q = q.reshape(B, T, SB_HEADS, SB_HEAD_DIM)
    k = k.reshape(B, T, SB_HEADS, SB_HEAD_DIM)
    v = v.reshape(B, T, SB_HEADS, SB_HEAD_DIM)
    if k_past is None:
        k_all, v_all, off = k, v, 0
    else:
        k_all = jnp.concatenate([k_past.astype(k.dtype), k], axis=1)
        v_all = jnp.concatenate([v_past.astype(v.dtype), v], axis=1)
        off = k_past.shape[1]
    o = stick_breaking_attention(q, k_all, v_all, sb_bias, off)
    return o.reshape(B, T, SB_WIDTH) @ w_out, k, v


def trunk(x, past, gla_state, conv_state, norm_g, w_in_even, w_gate_up, b_gate, gla_norm_g,
          conv_w, conv_b, conv_ln_g, conv_ln_b, w_out_even, w_qkv, sb_bias, w_out_odd, w_up, w_down):
    new_k, new_v, new_s, new_c = [], [], [], []
    for layer in range(DEPTH):
        i = layer // 2
        h = rmsnorm(x, norm_g[layer, 0])
        if layer % 2 == 0:
            m, s, c = even_mixer(h, gla_state[i], conv_state[i], w_in_even[i], w_gate_up[i], b_gate[i],
                                 gla_norm_g[i], conv_w[i], conv_b[i], conv_ln_g[i], conv_ln_b[i], w_out_even[i])
            new_s.append(s)
            new_c.append(c)
        else:
            if past is None:
                kp = vp = None
            else:
                cache_k, cache_v, page_table = past
                nb, n_pages = page_table.shape
                kp = cache_k[i][page_table].reshape(nb, n_pages * PAGE_SIZE, SB_HEADS, SB_HEAD_DIM)
                vp = cache_v[i][page_table].reshape(nb, n_pages * PAGE_SIZE, SB_HEADS, SB_HEAD_DIM)
            m, k, v = odd_mixer(h, kp, vp, w_qkv[i], sb_bias[i], w_out_odd[i])
            new_k.append(k)
            new_v.append(v)
        x = x + rmsnorm(m, norm_g[layer, 1])
        h = rmsnorm(x, norm_g[layer, 2])
        f = jnp.square(jax.nn.relu(h @ w_up[layer])) @ w_down[layer]
        x = x + rmsnorm(f, norm_g[layer, 3])
    return x, jnp.stack(new_k), jnp.stack(new_v), jnp.stack(new_s), jnp.stack(new_c)


def setup_inputs(seed: int = 0) -> dict:
    key = jax.random.key(seed)
    ks = jax.random.split(key, 24)
    f32 = jnp.float32
    n_pages = PAST_LEN // PAGE_SIZE
    n_used = DEC_BATCH * n_pages
    n_pool = n_used + max(1, n_used // 4)
    nrm = lambda k, shape, s: jax.random.normal(k, shape, f32) * s
    page_table = jax.random.permutation(ks[0], n_pool)[:n_used].reshape(DEC_BATCH, n_pages).astype(jnp.int32)
    return {
        'x_prompt': nrm(ks[1], (BATCH, SEQ, D_MODEL), 1.0),
        'x_sample': nrm(ks[2], (DEC_BATCH, DEC_SEQ, D_MODEL), 1.0),
        'cache_k': nrm(ks[3], (N_SB_LAYERS, n_pool, PAGE_SIZE, SB_HEADS, SB_HEAD_DIM), 1.0),
        'cache_v': nrm(ks[4], (N_SB_LAYERS, n_pool, PAGE_SIZE, SB_HEADS, SB_HEAD_DIM), 1.0),
        'state_gla': nrm(ks[5], (N_GLA_LAYERS, DEC_BATCH, GLA_HEADS, GLA_DK, GLA_DV), 1.0),
        'state_conv': nrm(ks[6], (N_GLA_LAYERS, DEC_BATCH, CONV_W - 1, CONV_CH), 0.5),
        'page_table': page_table,
        'meta_tokens': nrm(ks[7], (N_META, D_MODEL), 1.0),
        'norm_g': 1.0 + nrm(ks[8], (DEPTH, 4, D_MODEL), 0.02),
        'w_in_even': nrm(ks[9], (N_GLA_LAYERS, D_MODEL, IN_EVEN), D_MODEL ** -0.5),
        'w_gate_up': nrm(ks[10], (N_GLA_LAYERS, GLA_RANK, GLA_KEY), GLA_RANK ** -0.5),
        'b_gate': nrm(ks[11], (N_GLA_LAYERS, GLA_KEY), 0.02),
        'gla_norm_g': 1.0 + nrm(ks[12], (N_GLA_LAYERS, GLA_DV), 0.02),
        'conv_w': nrm(ks[13], (N_GLA_LAYERS, CONV_W, CONV_CH), CONV_W ** -0.5),
        'conv_b': nrm(ks[14], (N_GLA_LAYERS, CONV_CH), 0.02),
        'conv_ln_g': 1.0 + nrm(ks[15], (N_GLA_LAYERS, CONV_CH), 0.02),
        'conv_ln_b': nrm(ks[16], (N_GLA_LAYERS, CONV_CH), 0.02),
        'w_out_even': nrm(ks[17], (N_GLA_LAYERS, MIX_EVEN, D_MODEL), MIX_EVEN ** -0.5),
        'w_qkv': nrm(ks[18], (N_SB_LAYERS, D_MODEL, 3 * SB_WIDTH), D_MODEL ** -0.5),
        'sb_bias': SB_BIAS_INIT + nrm(ks[22], (N_SB_LAYERS, SB_HEADS), 0.1),
        'w_out_odd': nrm(ks[19], (N_SB_LAYERS, SB_WIDTH, D_MODEL), SB_WIDTH ** -0.5),
        'w_up': nrm(ks[20], (DEPTH, D_MODEL, D_FF), D_MODEL ** -0.5),
        'w_down': nrm(ks[21], (DEPTH, D_FF, D_MODEL), D_FF ** -0.5),
    }


def reference(x_prompt, x_sample, cache_k, cache_v, state_gla, state_conv, page_table, meta_tokens,
              norm_g, w_in_even, w_gate_up, b_gate, gla_norm_g, conv_w, conv_b, conv_ln_g, conv_ln_b,
              w_out_even, w_qkv, sb_bias, w_out_odd, w_up, w_down):
    weights = (norm_g, w_in_even, w_gate_up, b_gate, gla_norm_g, conv_w, conv_b, conv_ln_g, conv_ln_b,
               w_out_even, w_qkv, sb_bias, w_out_odd, w_up, w_down)
    B = x_prompt.shape[0]
    meta = jnp.broadcast_to(meta_tokens[None].astype(x_prompt.dtype), (B, N_META, D_MODEL))
    xp = jnp.concatenate([meta, x_prompt], axis=1)
    zero_gla = jnp.zeros((N_GLA_LAYERS, B, GLA_HEADS, GLA_DK, GLA_DV), x_prompt.dtype)
    zero_conv = jnp.zeros((N_GLA_LAYERS, B, CONV_W - 1, CONV_CH), x_prompt.dtype)
    yp, prompt_k, prompt_v, prompt_gla, prompt_conv = trunk(xp, None, zero_gla, zero_conv, *weights)
    ys, sample_k, sample_v, sample_gla, sample_conv = trunk(
        x_sample, (cache_k, cache_v, page_table), state_gla, state_conv, *weights)
    return (yp[:, N_META:], ys, prompt_k, prompt_v, sample_k, sample_v,
            prompt_gla, sample_gla, prompt_conv, sample_conv)
```

```python
import functools

import jax
import jax.numpy as jnp
from jax import lax
from jax.experimental import pallas as pl
from jax.experimental.pallas import tpu as pltpu

F32 = jnp.float32
BF16 = jnp.bfloat16

D_MODEL = 4096
N_META = 16
EPS = 1e-6
GLA_HEADS = 8
GLA_DK = 128
GLA_DV = 256
GLA_KEY = GLA_HEADS * GLA_DK
GLA_VAL = GLA_HEADS * GLA_DV
GLA_RANK = 16
GLA_TAU = 16.0
GLA_CHUNK = 16
CONV_CH = 2048
CONV_W = 31
SB_HEADS = 32
SB_HEAD_DIM = 128
PAGE_SIZE = 128
LANES = 128
CONV_PAD = 32
NEW_ROWS = 16
VMEM_LIMIT_BYTES = 60 * 1024 * 1024


def _cparams(*sem):
    return pltpu.CompilerParams(dimension_semantics=sem, vmem_limit_bytes=VMEM_LIMIT_BYTES)


def _sigmoid(x):
    return 1.0 / (1.0 + jnp.exp(-x))


def _softplus(x):
    return jnp.maximum(x, 0.0) + jnp.log1p(jnp.exp(-jnp.abs(x)))


def _mm_kernel(ap_ref, as_ref, w_ref, op_ref, os_ref, accp_ref, accs_ref, *, nk, act):
    i = pl.program_id(1)
    k = pl.program_id(2)
    w = w_ref[...].astype(BF16)

    def epilogue(x):
        if act == "relu2":
            r = jnp.maximum(x, 0.0)
            return r * r
        return x

    def accumulate(a_ref, acc_ref, o_ref):
        part = jnp.dot(a_ref[...], w, preferred_element_type=F32)
        if nk == 1:
            o_ref[...] = epilogue(part).astype(o_ref.dtype)
            return

        @pl.when(k == 0)
        def _():
            acc_ref[...] = part

        @pl.when((k > 0) & (k < nk - 1))
        def _():
            acc_ref[...] += part

        @pl.when(k == nk - 1)
        def _():
            o_ref[...] = epilogue(acc_ref[...] + part).astype(o_ref.dtype)

    accumulate(ap_ref, accp_ref, op_ref)

    @pl.when(i == 0)
    def _():
        accumulate(as_ref, accs_ref, os_ref)


def _matmul(ap, as_, w, layer, *, col0=0, ncols=None, act=None, dt_p=F32, dt_s=F32, tm, tn, tk):
    mp, kdim = ap.shape
    ms = as_.shape[0]
    ncols = w.shape[2] if ncols is None else ncols
    assert mp % tm == 0 and kdim % tk == 0 and ncols % tn == 0 and col0 % tn == 0
    nj, ni, nk = ncols // tn, mp // tm, kdim // tk
    jb = col0 // tn
    return pl.pallas_call(
        functools.partial(_mm_kernel, nk=nk, act=act),
        grid=(nj, ni, nk),
        in_specs=[
            pl.BlockSpec((tm, tk), lambda j, i, k: (i, k)),
            pl.BlockSpec((ms, tk), lambda j, i, k: (0, k)),
            pl.BlockSpec((None, tk, tn), lambda j, i, k: (layer, k, j + jb)),
        ],
        out_specs=[
            pl.BlockSpec((tm, tn), lambda j, i, k: (i, j)),
            pl.BlockSpec((ms, tn), lambda j, i, k: (0, j)),
        ],
        out_shape=[jax.ShapeDtypeStruct((mp, ncols), dt_p), jax.ShapeDtypeStruct((ms, ncols), dt_s)],
        scratch_shapes=[pltpu.VMEM((tm, tn), F32), pltpu.VMEM((ms, tn), F32)],
        compiler_params=_cparams("parallel", "arbitrary", "arbitrary"),
    )(ap, as_, w)


def _rms(x, g):
    return x * lax.rsqrt(jnp.mean(x * x, axis=-1, keepdims=True) + EPS) * g


def _norm_kernel(x_ref, g_ref, h_ref):
    h_ref[...] = _rms(x_ref[...], g_ref[...]).astype(h_ref.dtype)


def _resnorm_kernel(x_ref, m_ref, g1_ref, g2_ref, xo_ref, h_ref):
    x = x_ref[...] + _rms(m_ref[...], g1_ref[...])
    xo_ref[...] = x
    h_ref[...] = _rms(x, g2_ref[...]).astype(h_ref.dtype)


def _res_kernel(x_ref, m_ref, g1_ref, xo_ref):
    xo_ref[...] = x_ref[...] + _rms(m_ref[...], g1_ref[...])


def _row_spec(tr, d):
    return pl.BlockSpec((tr, d), lambda i: (i, 0))


def _vec_spec(d):
    return pl.BlockSpec((1, d), lambda i: (0, 0))


def _norm(x, g, *, tr):
    r, d = x.shape
    return pl.pallas_call(
        _norm_kernel, grid=(r // tr,),
        in_specs=[_row_spec(tr, d), _vec_spec(d)],
        out_specs=_row_spec(tr, d),
        out_shape=jax.ShapeDtypeStruct((r, d), BF16),
        compiler_params=_cparams("parallel"),
    )(x, g.reshape(1, d))


def _resnorm(x, m, g1, g2, *, tr):
    r, d = x.shape
    return pl.pallas_call(
        _resnorm_kernel, grid=(r // tr,),
        in_specs=[_row_spec(tr, d), _row_spec(tr, d), _vec_spec(d), _vec_spec(d)],
        out_specs=[_row_spec(tr, d), _row_spec(tr, d)],
        out_shape=[jax.ShapeDtypeStruct((r, d), F32), jax.ShapeDtypeStruct((r, d), BF16)],
        compiler_params=_cparams("parallel"),
    )(x, m, g1.reshape(1, d), g2.reshape(1, d))


def _res(x, m, g1, *, tr):
    r, d = x.shape
    return pl.pallas_call(
        _res_kernel, grid=(r // tr,),
        in_specs=[_row_spec(tr, d), _row_spec(tr, d), _vec_spec(d)],
        out_specs=_row_spec(tr, d),
        out_shape=jax.ShapeDtypeStruct((r, d), F32),
        compiler_params=_cparams("parallel"),
    )(x, m, g1.reshape(1, d))


def _gate_kernel(h_ref, w_ref, wg_ref, bg_ref, b_ref, *, chunk):
    g = jnp.dot(h_ref[...], w_ref[...].astype(BF16), preferred_element_type=F32)
    lane = lax.broadcasted_iota(jnp.int32, g.shape, 1)
    g = jnp.where(lane < GLA_RANK, g, 0.0)
    pre = jnp.dot(g.astype(BF16), wg_ref[...].astype(BF16), preferred_element_type=F32) + bg_ref[...]
    log_a = (jnp.minimum(pre, 0.0) - jnp.log1p(jnp.exp(-jnp.abs(pre)))) * (1.0 / GLA_TAU)
    tr = log_a.shape[0]
    shift = chunk.bit_length() - 1
    r = lax.broadcasted_iota(jnp.int32, (tr, tr), 0)
    c = lax.broadcasted_iota(jnp.int32, (tr, tr), 1)
    same = jnp.right_shift(r, shift) == jnp.right_shift(c, shift)
    tri = jnp.where(same & (c <= r), 1.0, 0.0).astype(F32)
    b_ref[...] = jnp.dot(tri, log_a, preferred_element_type=F32, precision=lax.Precision.HIGHEST)


def _gate(h, w_in, layer, wg_pad, b_gate, *, chunk, tr):
    r, d = h.shape
    gate_block = (2 * GLA_KEY + 2 * GLA_VAL) // LANES
    return pl.pallas_call(
        functools.partial(_gate_kernel, chunk=chunk), grid=(r // tr,),
        in_specs=[
            _row_spec(tr, d),
            pl.BlockSpec((None, d, LANES), lambda i: (layer, 0, gate_block)),
            pl.BlockSpec((LANES, GLA_KEY), lambda i: (0, 0)),
            _vec_spec(GLA_KEY),
        ],
        out_specs=_row_spec(tr, GLA_KEY),
        out_shape=jax.ShapeDtypeStruct((r, GLA_KEY), F32),
        compiler_params=_cparams("parallel"),
    )(h, w_in, wg_pad, b_gate.reshape(1, GLA_KEY))


def _gla_kernel(q_ref, k_ref, v_ref, r_ref, b_ref, s0_ref, g_ref, o_ref, sn_ref, st_ref, *, chunk, hb, nt):
    t = pl.program_id(2)
    nc = q_ref.shape[0] // chunk

    @pl.when(t == 0)
    def _():
        for hh in range(hb):
            st_ref[hh] = s0_ref[0, hh].T

    row = lax.broadcasted_iota(jnp.int32, (chunk, chunk, GLA_DK), 0)
    col = lax.broadcasted_iota(jnp.int32, (chunk, chunk, GLA_DK), 1)
    causal = col <= row
    gam = g_ref[...]
    nt_dims = (((1,), (1,)), ((), ()))
    tn_dims = (((0,), (0,)), ((), ()))

    def body(c, carry):
        r0 = pl.multiple_of(c * chunk, chunk)
        rows = pl.ds(r0, chunk)
        for hh in range(hb):
            kcols = slice(hh * GLA_DK, (hh + 1) * GLA_DK)
            vcols = slice(hh * GLA_DV, (hh + 1) * GLA_DV)
            q = q_ref[rows, kcols] * (GLA_DK ** -0.5)
            k = k_ref[rows, kcols]
            v = v_ref[rows, vcols]
            b = b_ref[rows, kcols]
            st = st_ref[hh]
            o_inter = lax.dot_general((q * jnp.exp(b)).astype(BF16), st.astype(BF16), nt_dims,
                                      preferred_element_type=F32)
            rel = jnp.where(causal, b[:, None, :] - b[None, :, :], -jnp.inf)
            att = jnp.sum(q[:, None, :] * jnp.exp(rel) * k[None, :, :], axis=-1)
            o_intra = jnp.dot(att.astype(BF16), v.astype(BF16), preferred_element_type=F32)
            b_last = b[chunk - 1:chunk, :]
            kd = k * jnp.exp(b_last - b)
            upd = lax.dot_general(v.astype(BF16), kd.astype(BF16), tn_dims, preferred_element_type=F32)
            st_ref[hh] = jnp.exp(b_last) * st + upd
            o = o_inter + o_intra
            y = o * lax.rsqrt(jnp.mean(o * o, axis=-1, keepdims=True) + EPS) * gam
            rr = r_ref[rows, vcols]
            o_ref[rows, vcols] = (y * (rr * _sigmoid(rr))).astype(o_ref.dtype)
        return carry

    lax.fori_loop(0, nc, body, 0)

    @pl.when(t == nt - 1)
    def _():
        for hh in range(hb):
            sn_ref[0, hh] = st_ref[hh].T


def _gla(proj, bcum, s0, gla_g, *, nb, t_len, chunk, hb, tb, out_dtype):
    nt = t_len // tb
    kw, vw = hb * GLA_DK, hb * GLA_DV
    k_off, v_off, r_off = GLA_KEY // kw, (2 * GLA_KEY) // vw, (2 * GLA_KEY + GLA_VAL) // vw
    rows = lambda b, h, t: b * nt + t
    return pl.pallas_call(
        functools.partial(_gla_kernel, chunk=chunk, hb=hb, nt=nt),
        grid=(nb, GLA_HEADS // hb, nt),
        in_specs=[
            pl.BlockSpec((tb, kw), lambda b, h, t: (rows(b, h, t), h)),
            pl.BlockSpec((tb, kw), lambda b, h, t: (rows(b, h, t), k_off + h)),
            pl.BlockSpec((tb, vw), lambda b, h, t: (rows(b, h, t), v_off + h)),
            pl.BlockSpec((tb, vw), lambda b, h, t: (rows(b, h, t), r_off + h)),
            pl.BlockSpec((tb, kw), lambda b, h, t: (rows(b, h, t), h)),
            pl.BlockSpec((1, hb, GLA_DK, GLA_DV), lambda b, h, t: (b, h, 0, 0)),
            pl.BlockSpec((1, GLA_DV), lambda b, h, t: (0, 0)),
        ],
        out_specs=[
            pl.BlockSpec((tb, vw), lambda b, h, t: (rows(b, h, t), h)),
            pl.BlockSpec((1, hb, GLA_DK, GLA_DV), lambda b, h, t: (b, h, 0, 0)),
        ],
        out_shape=[
            jax.ShapeDtypeStruct((nb * t_len, GLA_VAL), out_dtype),
            jax.ShapeDtypeStruct((nb, GLA_HEADS, GLA_DK, GLA_DV), F32),
        ],
        scratch_shapes=[pltpu.VMEM((hb, GLA_DV, GLA_DK), F32)],
        compiler_params=_cparams("parallel", "parallel", "arbitrary"),
    )(proj, proj, proj, proj, bcum, s0, gla_g.reshape(1, GLA_DV))


def _conv_kernel(a_ref, g_ref, buf_ref, w_ref, cb_ref, lg_ref, lb_ref, c_ref, st_ref, u_ref, *, rc, cc, nt):
    t = pl.program_id(1)
    tb = a_ref.shape[0]
    hist = CONV_W - 1

    @pl.when(t == 0)
    def _():
        u_ref[0:CONV_PAD, :] = jnp.zeros((CONV_PAD, CONV_CH), F32)
        u_ref[CONV_PAD - hist:CONV_PAD, :] = buf_ref[0]

    @pl.when(t > 0)
    def _():
        u_ref[0:CONV_PAD, :] = u_ref[tb:tb + CONV_PAD, :]

    u_ref[CONV_PAD:CONV_PAD + tb, :] = a_ref[...] * _sigmoid(g_ref[...])

    def body(i, carry):
        r0 = pl.multiple_of(i * rc, rc)
        pieces = []
        for c0 in range(0, CONV_CH, cc):
            win = u_ref[pl.ds(r0, rc + CONV_PAD), c0:c0 + cc]
            acc = jnp.zeros((rc, cc), F32)
            for tap in range(CONV_W):
                off = CONV_PAD - hist + tap
                acc = acc + win[off:off + rc, :] * w_ref[tap:tap + 1, c0:c0 + cc]
            pieces.append(acc + cb_ref[:, c0:c0 + cc])
        x = jnp.concatenate(pieces, axis=1)
        mu = jnp.mean(x, axis=-1, keepdims=True)
        xc = x - mu
        var = jnp.mean(xc * xc, axis=-1, keepdims=True)
        y = xc * lax.rsqrt(var + EPS) * lg_ref[...] + lb_ref[...]
        c_ref[pl.ds(r0, rc), :] = (y * _sigmoid(y)).astype(c_ref.dtype)
        return carry

    lax.fori_loop(0, tb // rc, body, 0)

    @pl.when(t == nt - 1)
    def _():
        st_ref[0] = u_ref[tb + CONV_PAD - hist:tb + CONV_PAD, :]


def _conv(glu, buf, conv_w, conv_b, ln_g, ln_b, *, nb, t_len, tb, rc, out_dtype):
    nt = t_len // tb
    vec = lambda: pl.BlockSpec((1, CONV_CH), lambda b, t: (0, 0))
    return pl.pallas_call(
        functools.partial(_conv_kernel, rc=rc, cc=4 * LANES, nt=nt),
        grid=(nb, nt),
        in_specs=[
            pl.BlockSpec((tb, CONV_CH), lambda b, t: (b * nt + t, 0)),
            pl.BlockSpec((tb, CONV_CH), lambda b, t: (b * nt + t, 1)),
            pl.BlockSpec((1, CONV_W - 1, CONV_CH), lambda b, t: (b, 0, 0)),
            pl.BlockSpec((CONV_W, CONV_CH), lambda b, t: (0, 0)),
            vec(), vec(), vec(),
        ],
        out_specs=[
            pl.BlockSpec((tb, CONV_CH), lambda b, t: (b * nt + t, 0)),
            pl.BlockSpec((1, CONV_W - 1, CONV_CH), lambda b, t: (b, 0, 0)),
        ],
        out_shape=[
            jax.ShapeDtypeStruct((nb * t_len, CONV_CH), out_dtype),
            jax.ShapeDtypeStruct((nb, CONV_W - 1, CONV_CH), F32),
        ],
        scratch_shapes=[pltpu.VMEM((tb + CONV_PAD, CONV_CH), F32)],
        compiler_params=_cparams("parallel", "arbitrary"),
    )(glu, glu, buf, conv_w, conv_b.reshape(1, CONV_CH), ln_g.reshape(1, CONV_CH), ln_b.reshape(1, CONV_CH))


def _sb_tile(q, kp_ref, vp_ref, nkb, q0, bias, tk, upper, ones):
    tq = q.shape[0]
    scale = SB_HEAD_DIM ** -0.5
    nt_dims = (((1,), (1,)), ((), ()))
    qpos = q0 + lax.broadcasted_iota(jnp.int32, (tq, tk), 0)
    kidx = lax.broadcasted_iota(jnp.int32, (tq, tk), 1)

    def body(jj, carry):
        c, acc = carry
        k0 = pl.multiple_of((nkb - 1 - jj) * tk, tk)
        kt = kp_ref[pl.ds(k0, tk), :]
        z = lax.dot_general(q, kt, nt_dims, preferred_element_type=F32) * scale + bias
        mask = (k0 + kidx) < qpos
        sp = jnp.where(mask, _softplus(z), 0.0)
        hi = sp.astype(BF16)
        lo = (sp - hi.astype(F32)).astype(BF16)
        cs = jnp.dot(hi, upper, preferred_element_type=F32) + jnp.dot(lo, upper, preferred_element_type=F32)
        tot = jnp.dot(hi, ones, preferred_element_type=F32) + jnp.dot(lo, ones, preferred_element_type=F32)
        crep = jnp.concatenate([c] * (tk // LANES), axis=1)
        w = jnp.where(mask, jnp.exp(z - sp - cs - crep), 0.0)
        acc = acc + jnp.dot(w.astype(BF16), vp_ref[pl.ds(k0, tk), :], preferred_element_type=F32)
        return c + tot, acc

    init = (jnp.zeros((tq, LANES), F32), jnp.zeros((tq, SB_HEAD_DIM), F32))
    _, acc = lax.fori_loop(0, nkb, body, init)
    return acc


def _sb_prompt_kernel(bias_ref, q_ref, k_ref, v_ref, o_ref, kp_ref, vp_ref, *, t_len, tq, tk):
    bias = bias_ref[pl.program_id(1)]
    tp = kp_ref.shape[0]
    kp_ref[0:t_len, :] = k_ref[...].astype(BF16)
    vp_ref[0:t_len, :] = v_ref[...].astype(BF16)
    if tp > t_len:
        kp_ref[t_len:tp, :] = jnp.zeros((tp - t_len, SB_HEAD_DIM), BF16)
        vp_ref[t_len:tp, :] = jnp.zeros((tp - t_len, SB_HEAD_DIM), BF16)
    r = lax.broadcasted_iota(jnp.int32, (tk, tk), 0)
    c = lax.broadcasted_iota(jnp.int32, (tk, tk), 1)
    upper = jnp.where(r > c, 1.0, 0.0).astype(BF16)
    ones = jnp.ones((tk, LANES), BF16)
    n_full, rem = t_len // tq, t_len % tq
    per_q = tq // tk

    def q_body(qi, carry):
        q0 = pl.multiple_of(qi * tq, tq)
        acc = _sb_tile(q_ref[pl.ds(q0, tq), :], kp_ref, vp_ref, (qi + 1) * per_q, q0, bias, tk, upper, ones)
        o_ref[pl.ds(q0, tq), :] = acc.astype(o_ref.dtype)
        return carry

    lax.fori_loop(0, n_full, q_body, 0)
    if rem:
        q0 = n_full * tq
        acc = _sb_tile(q_ref[q0:t_len, :], kp_ref, vp_ref, tp // tk, q0, bias, tk, upper, ones)
        o_ref[q0:t_len, :] = acc.astype(o_ref.dtype)


def _sb_prompt(q, k, v, bias, *, nb, t_len, tq, tk):
    assert tq % tk == 0
    tp = -(-t_len // tk) * tk
    spec = lambda: pl.BlockSpec((t_len, SB_HEAD_DIM), lambda b, h: (b, h))
    return pl.pallas_call(
        functools.partial(_sb_prompt_kernel, t_len=t_len, tq=tq, tk=tk),
        grid=(nb, SB_HEADS),
        in_specs=[pl.BlockSpec(memory_space=pltpu.SMEM), spec(), spec(), spec()],
        out_specs=spec(),
        out_shape=jax.ShapeDtypeStruct((nb * t_len, SB_HEADS * SB_HEAD_DIM), BF16),
        scratch_shapes=[pltpu.VMEM((tp, SB_HEAD_DIM), BF16), pltpu.VMEM((tp, SB_HEAD_DIM), BF16)],
        compiler_params=_cparams("parallel", "parallel"),
    )(bias, q, k, v)


def _sb_sample_kernel(pt_ref, q_ref, kn_ref, vn_ref, bias_ref, *rest, pp, tn, past_len):
    k_refs, v_refs = rest[:pp], rest[pp:2 * pp]
    o_ref, qbd_ref, kb_ref, vb_ref, acc_ref, c_ref = rest[2 * pp:]
    g = pl.program_id(1)
    ng = pl.num_programs(1)
    hq = SB_HEADS * tn
    width = SB_HEADS * SB_HEAD_DIM
    scale = SB_HEAD_DIM ** -0.5
    tn_dims = (((0,), (0,)), ((), ()))
    lane_t = jnp.bitwise_and(lax.broadcasted_iota(jnp.int32, (1, hq), 1), tn - 1)
    bias = bias_ref[...]

    def block(kb, vb, key_pos):
        n = kb.shape[0]
        z = jnp.dot(kb, qbd_ref[...], preferred_element_type=F32) * scale + bias
        mask = key_pos < (past_len + lane_t)
        sp = jnp.where(mask, _softplus(z), 0.0)
        hi = sp.astype(BF16)
        lo = (sp - hi.astype(F32)).astype(BF16)
        r = lax.broadcasted_iota(jnp.int32, (n, n), 0)
        c = lax.broadcasted_iota(jnp.int32, (n, n), 1)
        newer = jnp.where(c > r, 1.0, 0.0).astype(BF16)
        cs = jnp.dot(newer, hi, preferred_element_type=F32) + jnp.dot(newer, lo, preferred_element_type=F32)
        w = jnp.where(mask, jnp.exp(z - sp - cs - c_ref[...]), 0.0)
        acc_ref[...] += lax.dot_general(w.astype(BF16), vb, tn_dims, preferred_element_type=F32)
        c_ref[...] += jnp.sum(sp, axis=0, keepdims=True)

    @pl.when(g == 0)
    def _():
        qt = jnp.concatenate([q_ref[...]] * SB_HEADS, axis=0)
        rh = jnp.right_shift(lax.broadcasted_iota(jnp.int32, (hq, width), 0), tn.bit_length() - 1)
        ch = jnp.right_shift(lax.broadcasted_iota(jnp.int32, (hq, width), 1), SB_HEAD_DIM.bit_length() - 1)
        qbd_ref[...] = jnp.where(rh == ch, qt, 0.0).T.astype(BF16)
        acc_ref[...] = jnp.zeros_like(acc_ref)
        c_ref[...] = jnp.zeros_like(c_ref)
        pad = jnp.zeros((NEW_ROWS - tn, width), F32)
        pos = past_len + lax.broadcasted_iota(jnp.int32, (NEW_ROWS, 1), 0)
        block(jnp.concatenate([kn_ref[...], pad], axis=0).astype(BF16),
              jnp.concatenate([vn_ref[...], pad], axis=0).astype(BF16), pos)

    for s in range(pp):
        kb_ref[s * PAGE_SIZE:(s + 1) * PAGE_SIZE, :] = k_refs[s][...].astype(BF16)
        vb_ref[s * PAGE_SIZE:(s + 1) * PAGE_SIZE, :] = v_refs[s][...].astype(BF16)
    n = pp * PAGE_SIZE
    base = (ng - 1 - g) * n
    block(kb_ref[...], vb_ref[...], base + lax.broadcasted_iota(jnp.int32, (n, 1), 0))

    @pl.when(g == ng - 1)
    def _():
        for h in range(SB_HEADS):
            cols = slice(h * SB_HEAD_DIM, (h + 1) * SB_HEAD_DIM)
            o_ref[:, cols] = acc_ref[h * tn:(h + 1) * tn, cols]


def _sb_sample(q, k_new, v_new, cache_k, cache_v, page_table, bias, *, layer, pp):
    nb, n_pages = page_table.shape
    tn = q.shape[0] // nb
    width = SB_HEADS * SB_HEAD_DIM
    hq = SB_HEADS * tn
    ng = n_pages // pp
    assert n_pages % pp == 0
    ck = cache_k.reshape(cache_k.shape[0], cache_k.shape[1], PAGE_SIZE, width)
    cv = cache_v.reshape(cache_v.shape[0], cache_v.shape[1], PAGE_SIZE, width)
    bias_lane = jnp.repeat(bias, tn).reshape(1, hq)
    row = lambda: pl.BlockSpec((tn, width), lambda b, g, pt: (b, 0))

    def page(s):
        return pl.BlockSpec((None, None, PAGE_SIZE, width),
                            lambda b, g, pt: (layer, pt[b, (ng - 1 - g) * pp + s], 0, 0))

    grid_spec = pltpu.PrefetchScalarGridSpec(
        num_scalar_prefetch=1,
        grid=(nb, ng),
        in_specs=[row(), row(), row(), pl.BlockSpec((1, hq), lambda b, g, pt: (0, 0))]
        + [page(s) for s in range(pp)] + [page(s) for s in range(pp)],
        out_specs=row(),
        scratch_shapes=[
            pltpu.VMEM((width, hq), BF16),
            pltpu.VMEM((pp * PAGE_SIZE, width), BF16),
            pltpu.VMEM((pp * PAGE_SIZE, width), BF16),
            pltpu.VMEM((hq, width), F32),
            pltpu.VMEM((1, hq), F32),
        ],
    )
    return pl.pallas_call(
        functools.partial(_sb_sample_kernel, pp=pp, tn=tn, past_len=n_pages * PAGE_SIZE),
        grid_spec=grid_spec,
        out_shape=jax.ShapeDtypeStruct((nb * tn, width), F32),
        compiler_params=_cparams("parallel", "arbitrary"),
    )(page_table, q, k_new, v_new, bias_lane, *([ck] * pp), *([cv] * pp))


MM = dict(tm=2064, tn=1024, tk=512)
ROWS_P = 192


def kernel(x_prompt, x_sample, cache_k, cache_v, state_gla, state_conv, page_table, meta_tokens,
           norm_g, w_in_even, w_gate_up, b_gate, gla_norm_g, conv_w, conv_b, conv_ln_g, conv_ln_b,
           w_out_even, w_qkv, sb_bias, w_out_odd, w_up, w_down):
    nb_p, seq, d = x_prompt.shape
    nb_s, seq_s, _ = x_sample.shape
    t_p = seq + N_META
    rows_s = nb_s * seq_s
    depth = norm_g.shape[0]

    meta = jnp.broadcast_to(meta_tokens[None], (nb_p, N_META, d))
    xp = jnp.concatenate([meta, x_prompt], axis=1).reshape(nb_p * t_p, d)
    xs = x_sample.reshape(rows_s, d)

    hp = _norm(xp, norm_g[0, 0], tr=ROWS_P)
    hs = _norm(xs, norm_g[0, 0], tr=rows_s)

    zero_gla = jnp.zeros((nb_p, GLA_HEADS, GLA_DK, GLA_DV), F32)
    zero_conv = jnp.zeros((nb_p, CONV_W - 1, CONV_CH), F32)
    glu_col0 = 2 * GLA_KEY + 2 * GLA_VAL + GLA_RANK

    pk, pv, sk, sv, pg, sg, pc, sc = [], [], [], [], [], [], [], []
    for layer in range(depth):
        i = layer // 2
        if layer % 2 == 0:
            proj_p, proj_s = _matmul(hp, hs, w_in_even, i, ncols=glu_col0 - GLA_RANK, **MM)
            glu_p, glu_s = _matmul(hp, hs, w_in_even[i:i + 1, :, glu_col0:], 0, **MM)
            wg_pad = jnp.zeros((LANES, GLA_KEY), F32).at[:GLA_RANK].set(w_gate_up[i])
            b_p = _gate(hp, w_in_even, i, wg_pad, b_gate[i], chunk=GLA_CHUNK, tr=ROWS_P)
            b_s = _gate(hs, w_in_even, i, wg_pad, b_gate[i], chunk=seq_s, tr=rows_s)
            o_p, g_p = _gla(proj_p, b_p, zero_gla, gla_norm_g[i], nb=nb_p, t_len=t_p, chunk=GLA_CHUNK,
                            hb=4, tb=t_p // 3, out_dtype=BF16)
            o_s, g_s = _gla(proj_s, b_s, state_gla[i], gla_norm_g[i], nb=nb_s, t_len=seq_s, chunk=seq_s,
                            hb=4, tb=seq_s, out_dtype=F32)
            c_p, cs_p = _conv(glu_p, zero_conv, conv_w[i], conv_b[i], conv_ln_g[i], conv_ln_b[i],
                              nb=nb_p, t_len=t_p, tb=t_p // 3, rc=16, out_dtype=BF16)
            c_s, cs_s = _conv(glu_s, state_conv[i], conv_w[i], conv_b[i], conv_ln_g[i], conv_ln_b[i],
                              nb=nb_s, t_len=seq_s, tb=seq_s, rc=seq_s, out_dtype=F32)
            mix_p = jnp.concatenate([o_p, c_p], axis=1)
            mix_s = jnp.concatenate([o_s, c_s], axis=1).astype(BF16)
            m_p, m_s = _matmul(mix_p, mix_s, w_out_even, i, **MM)
            pg.append(g_p); sg.append(g_s); pc.append(cs_p); sc.append(cs_s)
        else:
            width = SB_HEADS * SB_HEAD_DIM
            q_p, q_s = _matmul(hp, hs, w_qkv, i, col0=0, ncols=width, dt_p=BF16, **MM)
            k_p, k_s = _matmul(hp, hs, w_qkv, i, col0=width, ncols=width, **MM)
            v_p, v_s = _matmul(hp, hs, w_qkv, i, col0=2 * width, ncols=width, **MM)
            a_p = _sb_prompt(q_p, k_p, v_p, sb_bias[i], nb=nb_p, t_len=t_p, tq=256, tk=256)
            a_s = _sb_sample(q_s, k_s, v_s, cache_k, cache_v, page_table, sb_bias[i], layer=i, pp=2)
            m_p, m_s = _matmul(a_p, a_s.astype(BF16), w_out_odd, i, **MM)
            pk.append(k_p); pv.append(v_p); sk.append(k_s); sv.append(v_s)

        xp, hp = _resnorm(xp, m_p, norm_g[layer, 1], norm_g[layer, 2], tr=ROWS_P)
        xs, hs = _resnorm(xs, m_s, norm_g[layer, 1], norm_g[layer, 2], tr=rows_s)
        up_p, up_s = _matmul(hp, hs, w_up, layer, act="relu2", dt_p=BF16, dt_s=BF16, **MM)
        f_p, f_s = _matmul(up_p, up_s, w_down, layer, **MM)
        if layer + 1 < depth:
            xp, hp = _resnorm(xp, f_p, norm_g[layer, 3], norm_g[layer + 1, 0], tr=ROWS_P)
            xs, hs = _resnorm(xs, f_s, norm_g[layer, 3], norm_g[layer + 1, 0], tr=rows_s)
        else:
            xp = _res(xp, f_p, norm_g[layer, 3], tr=ROWS_P)
            xs = _res(xs, f_s, norm_g[layer, 3], tr=rows_s)

    kv_p = lambda a: jnp.stack(a).reshape(len(a), nb_p, t_p, SB_HEADS, SB_HEAD_DIM)
    kv_s = lambda a: jnp.stack(a).reshape(len(a), nb_s, seq_s, SB_HEADS, SB_HEAD_DIM)
    y_p = xp.reshape(nb_p, t_p, d)[:, N_META:]
    y_s = xs.reshape(nb_s, seq_s, d)
    return (y_p, y_s, kv_p(pk), kv_p(pv), kv_s(sk), kv_s(sv),
            jnp.stack(pg), jnp.stack(sg), jnp.stack(pc), jnp.stack(sc))
```

```python
import functools

import jax
import jax.numpy as jnp
from jax import lax
from jax.experimental import pallas as pl
from jax.experimental.pallas import tpu as pltpu

F32 = jnp.float32
BF16 = jnp.bfloat16

D_MODEL = 4096
N_META = 16
EPS = 1e-6
GLA_HEADS = 8
GLA_DK = 128
GLA_DV = 256
GLA_KEY = GLA_HEADS * GLA_DK
GLA_VAL = GLA_HEADS * GLA_DV
GLA_RANK = 16
GLA_TAU = 16.0
GLA_CHUNK = 16
CONV_CH = 2048
CONV_W = 31
SB_HEADS = 32
SB_HEAD_DIM = 128
PAGE_SIZE = 128
LANES = 128
CONV_PAD = 32
NEW_ROWS = 16
VMEM_LIMIT_BYTES = 60 * 1024 * 1024


def _cparams(*sem):
    return pltpu.CompilerParams(dimension_semantics=sem, vmem_limit_bytes=VMEM_LIMIT_BYTES)


def _sigmoid(x):
    return 1.0 / (1.0 + jnp.exp(-x))


def _softplus(x):
    return jnp.maximum(x, 0.0) + jnp.log(1.0 + jnp.exp(-jnp.abs(x)))


def _mm_kernel(ap_ref, as_ref, w_ref, op_ref, os_ref, accp_ref, accs_ref, *, nk, act):
    i = pl.program_id(1)
    k = pl.program_id(2)
    w = w_ref[...].astype(BF16)

    def epilogue(x):
        if act == "relu2":
            r = jnp.maximum(x, 0.0)
            return r * r
        return x

    def accumulate(a_ref, acc_ref, o_ref):
        part = lambda: jnp.dot(a_ref[...], w, preferred_element_type=F32)
        if nk == 1:
            o_ref[...] = epilogue(part()).astype(o_ref.dtype)
            return
        if act is None and o_ref.dtype == F32:
            @pl.when(k == 0)
            def _():
                o_ref[...] = part()

            @pl.when(k > 0)
            def _():
                o_ref[...] += part()
            return

        @pl.when(k == 0)
        def _():
            acc_ref[...] = part()

        @pl.when((k > 0) & (k < nk - 1))
        def _():
            acc_ref[...] += part()

        @pl.when(k == nk - 1)
        def _():
            o_ref[...] = epilogue(acc_ref[...] + part()).astype(o_ref.dtype)

    accumulate(ap_ref, accp_ref, op_ref)

    @pl.when(i == 0)
    def _():
        accumulate(as_ref, accs_ref, os_ref)


def _matmul(ap, as_, w, layer, *, col0=0, ncols=None, act=None, dt_p=F32, dt_s=F32, tm, tn, tk):
    mp, kdim = ap.shape
    ms = as_.shape[0]
    ncols = w.shape[2] if ncols is None else ncols
    assert mp % tm == 0 and kdim % tk == 0 and ncols % tn == 0 and col0 % tn == 0
    nj, ni, nk = ncols // tn, mp // tm, kdim // tk
    jb = col0 // tn

    def acc_shape(rows, dt):
        direct = nk == 1 or (act is None and dt == F32)
        return (8, LANES) if direct else (rows, tn)

    return pl.pallas_call(
        functools.partial(_mm_kernel, nk=nk, act=act),
        grid=(nj, ni, nk),
        in_specs=[
            pl.BlockSpec((tm, tk), lambda j, i, k: (i, k)),
            pl.BlockSpec((ms, tk), lambda j, i, k: (0, k)),
            pl.BlockSpec((None, tk, tn), lambda j, i, k: (layer, k, j + jb)),
        ],
        out_specs=[
            pl.BlockSpec((tm, tn), lambda j, i, k: (i, j)),
            pl.BlockSpec((ms, tn), lambda j, i, k: (0, j)),
        ],
        out_shape=[jax.ShapeDtypeStruct((mp, ncols), dt_p), jax.ShapeDtypeStruct((ms, ncols), dt_s)],
        scratch_shapes=[pltpu.VMEM(acc_shape(tm, dt_p), F32), pltpu.VMEM(acc_shape(ms, dt_s), F32)],
        compiler_params=_cparams("parallel", "arbitrary", "arbitrary"),
    )(ap, as_, w)


def _rms(x, g):
    return x * lax.rsqrt(jnp.mean(x * x, axis=-1, keepdims=True) + EPS) * g


def _norm_kernel(x_ref, g_ref, h_ref):
    h_ref[...] = _rms(x_ref[...], g_ref[...]).astype(h_ref.dtype)


def _resnorm_kernel(x_ref, m_ref, g1_ref, g2_ref, xo_ref, h_ref):
    x = x_ref[...] + _rms(m_ref[...], g1_ref[...])
    xo_ref[...] = x
    h_ref[...] = _rms(x, g2_ref[...]).astype(h_ref.dtype)


def _res_kernel(x_ref, m_ref, g1_ref, xo_ref):
    xo_ref[...] = x_ref[...] + _rms(m_ref[...], g1_ref[...])


def _row_spec(tr, d):
    return pl.BlockSpec((tr, d), lambda i: (i, 0))


def _vec_spec(d):
    return pl.BlockSpec((1, d), lambda i: (0, 0))


def _norm(x, g, *, tr):
    r, d = x.shape
    return pl.pallas_call(
        _norm_kernel, grid=(r // tr,),
        in_specs=[_row_spec(tr, d), _vec_spec(d)],
        out_specs=_row_spec(tr, d),
        out_shape=jax.ShapeDtypeStruct((r, d), BF16),
        compiler_params=_cparams("parallel"),
    )(x, g.reshape(1, d))


def _resnorm(x, m, g1, g2, *, tr):
    r, d = x.shape
    return pl.pallas_call(
        _resnorm_kernel, grid=(r // tr,),
        in_specs=[_row_spec(tr, d), _row_spec(tr, d), _vec_spec(d), _vec_spec(d)],
        out_specs=[_row_spec(tr, d), _row_spec(tr, d)],
        out_shape=[jax.ShapeDtypeStruct((r, d), F32), jax.ShapeDtypeStruct((r, d), BF16)],
        compiler_params=_cparams("parallel"),
    )(x, m, g1.reshape(1, d), g2.reshape(1, d))


def _res(x, m, g1, *, tr):
    r, d = x.shape
    return pl.pallas_call(
        _res_kernel, grid=(r // tr,),
        in_specs=[_row_spec(tr, d), _row_spec(tr, d), _vec_spec(d)],
        out_specs=_row_spec(tr, d),
        out_shape=jax.ShapeDtypeStruct((r, d), F32),
        compiler_params=_cparams("parallel"),
    )(x, m, g1.reshape(1, d))


def _gate_kernel(h_ref, w_ref, wg_ref, bg_ref, b_ref, *, chunk):
    g = jnp.dot(h_ref[...], w_ref[...].astype(BF16), preferred_element_type=F32)
    lane = lax.broadcasted_iota(jnp.int32, g.shape, 1)
    g = jnp.where(lane < GLA_RANK, g, 0.0)
    pre = jnp.dot(g.astype(BF16), wg_ref[...].astype(BF16), preferred_element_type=F32) + bg_ref[...]
    log_a = (jnp.minimum(pre, 0.0) - jnp.log1p(jnp.exp(-jnp.abs(pre)))) * (1.0 / GLA_TAU)
    tr = log_a.shape[0]
    shift = chunk.bit_length() - 1
    r = lax.broadcasted_iota(jnp.int32, (tr, tr), 0)
    c = lax.broadcasted_iota(jnp.int32, (tr, tr), 1)
    same = jnp.right_shift(r, shift) == jnp.right_shift(c, shift)
    tri = jnp.where(same & (c <= r), 1.0, 0.0).astype(F32)
    b_ref[...] = jnp.dot(tri, log_a, preferred_element_type=F32, precision=lax.Precision.HIGHEST)


def _gate(h, w_in, layer, wg_pad, b_gate, *, chunk, tr):
    r, d = h.shape
    gate_block = (2 * GLA_KEY + 2 * GLA_VAL) // LANES
    return pl.pallas_call(
        functools.partial(_gate_kernel, chunk=chunk), grid=(r // tr,),
        in_specs=[
            _row_spec(tr, d),
            pl.BlockSpec((None, d, LANES), lambda i: (layer, 0, gate_block)),
            pl.BlockSpec((LANES, GLA_KEY), lambda i: (0, 0)),
            _vec_spec(GLA_KEY),
        ],
        out_specs=_row_spec(tr, GLA_KEY),
        out_shape=jax.ShapeDtypeStruct((r, GLA_KEY), F32),
        compiler_params=_cparams("parallel"),
    )(h, w_in, wg_pad, b_gate.reshape(1, GLA_KEY))


def _gla_kernel(q_ref, k_ref, v_ref, r_ref, b_ref, s0_ref, g_ref, o_ref, sn_ref, st_ref, *, chunk, hb, nt):
    t = pl.program_id(2)
    nc = q_ref.shape[0] // chunk

    @pl.when(t == 0)
    def _():
        for hh in range(hb):
            st_ref[hh] = s0_ref[0, hh].T

    row = lax.broadcasted_iota(jnp.int32, (chunk, chunk, GLA_DK), 0)
    col = lax.broadcasted_iota(jnp.int32, (chunk, chunk, GLA_DK), 1)
    causal = col <= row
    gam = g_ref[...]
    nt_dims = (((1,), (1,)), ((), ()))
    tn_dims = (((0,), (0,)), ((), ()))

    def body(c, carry):
        r0 = pl.multiple_of(c * chunk, chunk)
        rows = pl.ds(r0, chunk)
        for hh in range(hb):
            kcols = slice(hh * GLA_DK, (hh + 1) * GLA_DK)
            vcols = slice(hh * GLA_DV, (hh + 1) * GLA_DV)
            q = q_ref[rows, kcols] * (GLA_DK ** -0.5)
            k = k_ref[rows, kcols]
            v = v_ref[rows, vcols]
            b = b_ref[rows, kcols]
            st = st_ref[hh]
            o_inter = lax.dot_general((q * jnp.exp(b)).astype(BF16), st.astype(BF16), nt_dims,
                                      preferred_element_type=F32)
            rel = jnp.where(causal, b[:, None, :] - b[None, :, :], -jnp.inf)
            att = jnp.sum(q[:, None, :] * jnp.exp(rel) * k[None, :, :], axis=-1)
            o_intra = jnp.dot(att.astype(BF16), v.astype(BF16), preferred_element_type=F32)
            b_last = b[chunk - 1:chunk, :]
            kd = k * jnp.exp(b_last - b)
            upd = lax.dot_general(v.astype(BF16), kd.astype(BF16), tn_dims, preferred_element_type=F32)
            st_ref[hh] = jnp.exp(b_last) * st + upd
            o = o_inter + o_intra
            y = o * lax.rsqrt(jnp.mean(o * o, axis=-1, keepdims=True) + EPS) * gam
            rr = r_ref[rows, vcols]
            o_ref[rows, vcols] = (y * (rr * _sigmoid(rr))).astype(o_ref.dtype)
        return carry

    lax.fori_loop(0, nc, body, 0)

    @pl.when(t == nt - 1)
    def _():
        for hh in range(hb):
            sn_ref[0, hh] = st_ref[hh].T


def _gla(proj, bcum, s0, gla_g, *, nb, t_len, chunk, hb, tb, out_dtype):
    nt = t_len // tb
    kw, vw = hb * GLA_DK, hb * GLA_DV
    k_off, v_off, r_off = GLA_KEY // kw, (2 * GLA_KEY) // vw, (2 * GLA_KEY + GLA_VAL) // vw
    rows = lambda b, h, t: b * nt + t
    return pl.pallas_call(
        functools.partial(_gla_kernel, chunk=chunk, hb=hb, nt=nt),
        grid=(nb, GLA_HEADS // hb, nt),
        in_specs=[
            pl.BlockSpec((tb, kw), lambda b, h, t: (rows(b, h, t), h)),
            pl.BlockSpec((tb, kw), lambda b, h, t: (rows(b, h, t), k_off + h)),
            pl.BlockSpec((tb, vw), lambda b, h, t: (rows(b, h, t), v_off + h)),
            pl.BlockSpec((tb, vw), lambda b, h, t: (rows(b, h, t), r_off + h)),
            pl.BlockSpec((tb, kw), lambda b, h, t: (rows(b, h, t), h)),
            pl.BlockSpec((1, hb, GLA_DK, GLA_DV), lambda b, h, t: (b, h, 0, 0)),
            pl.BlockSpec((1, GLA_DV), lambda b, h, t: (0, 0)),
        ],
        out_specs=[
            pl.BlockSpec((tb, vw), lambda b, h, t: (rows(b, h, t), h)),
            pl.BlockSpec((1, hb, GLA_DK, GLA_DV), lambda b, h, t: (b, h, 0, 0)),
        ],
        out_shape=[
            jax.ShapeDtypeStruct((nb * t_len, GLA_VAL), out_dtype),
            jax.ShapeDtypeStruct((nb, GLA_HEADS, GLA_DK, GLA_DV), F32),
        ],
        scratch_shapes=[pltpu.VMEM((hb, GLA_DV, GLA_DK), F32)],
        compiler_params=_cparams("parallel", "parallel", "arbitrary"),
    )(proj, proj, proj, proj, bcum, s0, gla_g.reshape(1, GLA_DV))


def _conv_kernel(a_ref, g_ref, buf_ref, w_ref, cb_ref, lg_ref, lb_ref, c_ref, st_ref, u_ref, *, rc, cc, nt):
    t = pl.program_id(1)
    tb = a_ref.shape[0]
    hist = CONV_W - 1

    @pl.when(t == 0)
    def _():
        u_ref[0:CONV_PAD, :] = jnp.zeros((CONV_PAD, CONV_CH), F32)
        u_ref[CONV_PAD - hist:CONV_PAD, :] = buf_ref[0]

    @pl.when(t > 0)
    def _():
        u_ref[0:CONV_PAD, :] = u_ref[tb:tb + CONV_PAD, :]

    u_ref[CONV_PAD:CONV_PAD + tb, :] = a_ref[...] * _sigmoid(g_ref[...])

    def body(i, carry):
        r0 = pl.multiple_of(i * rc, rc)
        pieces = []
        for c0 in range(0, CONV_CH, cc):
            win = u_ref[pl.ds(r0, rc + CONV_PAD), c0:c0 + cc]
            acc = jnp.zeros((rc, cc), F32)
            for tap in range(CONV_W):
                off = CONV_PAD - hist + tap
                acc = acc + win[off:off + rc, :] * w_ref[tap:tap + 1, c0:c0 + cc]
            pieces.append(acc + cb_ref[:, c0:c0 + cc])
        x = jnp.concatenate(pieces, axis=1)
        mu = jnp.mean(x, axis=-1, keepdims=True)
        xc = x - mu
        var = jnp.mean(xc * xc, axis=-1, keepdims=True)
        y = xc * lax.rsqrt(var + EPS) * lg_ref[...] + lb_ref[...]
        c_ref[pl.ds(r0, rc), :] = (y * _sigmoid(y)).astype(c_ref.dtype)
        return carry

    lax.fori_loop(0, tb // rc, body, 0)

    @pl.when(t == nt - 1)
    def _():
        st_ref[0] = u_ref[tb + CONV_PAD - hist:tb + CONV_PAD, :]


def _conv(glu, buf, conv_w, conv_b, ln_g, ln_b, *, nb, t_len, tb, rc, out_dtype):
    nt = t_len // tb
    vec = lambda: pl.BlockSpec((1, CONV_CH), lambda b, t: (0, 0))
    return pl.pallas_call(
        functools.partial(_conv_kernel, rc=rc, cc=4 * LANES, nt=nt),
        grid=(nb, nt),
        in_specs=[
            pl.BlockSpec((tb, CONV_CH), lambda b, t: (b * nt + t, 0)),
            pl.BlockSpec((tb, CONV_CH), lambda b, t: (b * nt + t, 1)),
            pl.BlockSpec((1, CONV_W - 1, CONV_CH), lambda b, t: (b, 0, 0)),
            pl.BlockSpec((CONV_W, CONV_CH), lambda b, t: (0, 0)),
            vec(), vec(), vec(),
        ],
        out_specs=[
            pl.BlockSpec((tb, CONV_CH), lambda b, t: (b * nt + t, 0)),
            pl.BlockSpec((1, CONV_W - 1, CONV_CH), lambda b, t: (b, 0, 0)),
        ],
        out_shape=[
            jax.ShapeDtypeStruct((nb * t_len, CONV_CH), out_dtype),
            jax.ShapeDtypeStruct((nb, CONV_W - 1, CONV_CH), F32),
        ],
        scratch_shapes=[pltpu.VMEM((tb + CONV_PAD, CONV_CH), F32)],
        compiler_params=_cparams("parallel", "arbitrary"),
    )(glu, glu, buf, conv_w, conv_b.reshape(1, CONV_CH), ln_g.reshape(1, CONV_CH), ln_b.reshape(1, CONV_CH))


def _sb_prompt_kernel(bias_ref, q_ref, k_ref, v_ref, o_ref, kp_ref, vp_ref, *, t_len, tile):
    bias = bias_ref[pl.program_id(1)]
    tp = kp_ref.shape[0]
    kp_ref[0:t_len, :] = k_ref[...].astype(BF16)
    vp_ref[0:t_len, :] = v_ref[...].astype(BF16)
    if tp > t_len:
        kp_ref[t_len:tp, :] = jnp.zeros((tp - t_len, SB_HEAD_DIM), BF16)
        vp_ref[t_len:tp, :] = jnp.zeros((tp - t_len, SB_HEAD_DIM), BF16)
    r = lax.broadcasted_iota(jnp.int32, (2 * tile, tile), 0)
    c = lax.broadcasted_iota(jnp.int32, (2 * tile, tile), 1)
    newer2 = jnp.where(jnp.bitwise_and(r, tile - 1) > c, 1.0, 0.0).astype(BF16)
    for q0 in range(0, t_len, tile):
        q1 = min(q0 + tile, t_len)
        o_ref[q0:q1, :] = _sb_rows(q_ref[q0:q1, :], kp_ref, vp_ref, q0 // tile + 1, tile, bias, newer2)


def _sb_rows(q, kp_ref, vp_ref, n, tile, bias, newer2):
    tq = q.shape[0]
    nt_dims = (((1,), (1,)), ((), ()))
    blk = lambda a, j: a[:, j * tile:(j + 1) * tile]
    z = lax.dot_general(q, kp_ref[0:n * tile, :], nt_dims, preferred_element_type=F32)
    z = z * (SB_HEAD_DIM ** -0.5) + bias
    mask = lax.broadcasted_iota(jnp.int32, (tq, tile), 1) < lax.broadcasted_iota(jnp.int32, (tq, tile), 0)
    sp_all = _softplus(z)
    sp = [blk(sp_all, j) for j in range(n)]
    sp[-1] = jnp.where(mask, sp[-1], 0.0)
    hi = [s.astype(BF16) for s in sp]
    lo = [(s - h.astype(F32)).astype(BF16) for s, h in zip(sp, hi)]
    hilo = jnp.concatenate([jnp.concatenate([h, l], axis=1) for h, l in zip(hi, lo)], axis=0)
    cs = jnp.dot(hilo, newer2, preferred_element_type=F32)
    c = [None] * n
    run = jnp.zeros((tq, 1), F32)
    for j in range(n - 1, -1, -1):
        c[j] = run
        run = run + jnp.sum(sp[j], axis=1, keepdims=True)
    w = []
    for j in range(n):
        wj = jnp.exp(blk(z, j) - sp[j] - cs[j * tq:(j + 1) * tq] - c[j])
        if j == n - 1:
            wj = jnp.where(mask, wj, 0.0)
        w.append(wj.astype(BF16))
    out = jnp.dot(jnp.concatenate(w, axis=1), vp_ref[0:n * tile, :], preferred_element_type=F32)
    return out.astype(BF16)


def _sb_prompt(q, k, v, bias, *, nb, t_len, tile):
    tp = -(-t_len // tile) * tile
    spec = lambda: pl.BlockSpec((t_len, SB_HEAD_DIM), lambda b, h: (b, h))
    return pl.pallas_call(
        functools.partial(_sb_prompt_kernel, t_len=t_len, tile=tile),
        grid=(nb, SB_HEADS),
        in_specs=[pl.BlockSpec(memory_space=pltpu.SMEM), spec(), spec(), spec()],
        out_specs=spec(),
        out_shape=jax.ShapeDtypeStruct((nb * t_len, SB_HEADS * SB_HEAD_DIM), BF16),
        scratch_shapes=[pltpu.VMEM((tp, SB_HEAD_DIM), BF16), pltpu.VMEM((tp, SB_HEAD_DIM), BF16)],
        compiler_params=_cparams("parallel", "parallel"),
    )(bias, q, k, v)


def _sb_sample_kernel(pt_ref, q_ref, kn_ref, vn_ref, bias_ref, *rest, pp, tn, past_len):
    k_refs, v_refs = rest[:pp], rest[pp:2 * pp]
    o_ref, qbd_ref, kb_ref, vb_ref, acc_ref, c_ref = rest[2 * pp:]
    g = pl.program_id(1)
    ng = pl.num_programs(1)
    hq = SB_HEADS * tn
    width = SB_HEADS * SB_HEAD_DIM
    scale = SB_HEAD_DIM ** -0.5
    tn_dims = (((0,), (0,)), ((), ()))
    lane_t = jnp.bitwise_and(lax.broadcasted_iota(jnp.int32, (1, hq), 1), tn - 1)
    bias = bias_ref[...]

    def block(kb, vb, key_pos):
        n = kb.shape[0]
        z = jnp.dot(kb, qbd_ref[...], preferred_element_type=F32) * scale + bias
        mask = key_pos < (past_len + lane_t)
        sp = jnp.where(mask, _softplus(z), 0.0)
        hi = sp.astype(BF16)
        lo = (sp - hi.astype(F32)).astype(BF16)
        r = lax.broadcasted_iota(jnp.int32, (n, n), 0)
        c = lax.broadcasted_iota(jnp.int32, (n, n), 1)
        newer = jnp.where(c > r, 1.0, 0.0).astype(BF16)
        cs = jnp.dot(newer, hi, preferred_element_type=F32) + jnp.dot(newer, lo, preferred_element_type=F32)
        w = jnp.where(mask, jnp.exp(z - sp - cs - c_ref[...]), 0.0)
        acc_ref[...] += lax.dot_general(w.astype(BF16), vb, tn_dims, preferred_element_type=F32)
        c_ref[...] += jnp.sum(sp, axis=0, keepdims=True)

    @pl.when(g == 0)
    def _():
        qt = jnp.concatenate([q_ref[...]] * SB_HEADS, axis=0)
        rh = jnp.right_shift(lax.broadcasted_iota(jnp.int32, (hq, width), 0), tn.bit_length() - 1)
        ch = jnp.right_shift(lax.broadcasted_iota(jnp.int32, (hq, width), 1), SB_HEAD_DIM.bit_length() - 1)
        qbd_ref[...] = jnp.where(rh == ch, qt, 0.0).T.astype(BF16)
        acc_ref[...] = jnp.zeros_like(acc_ref)
        c_ref[...] = jnp.zeros_like(c_ref)
        pad = jnp.zeros((NEW_ROWS - tn, width), F32)
        pos = past_len + lax.broadcasted_iota(jnp.int32, (NEW_ROWS, 1), 0)
        block(jnp.concatenate([kn_ref[...], pad], axis=0).astype(BF16),
              jnp.concatenate([vn_ref[...], pad], axis=0).astype(BF16), pos)

    for s in range(pp):
        rows = slice(s * PAGE_SIZE, (s + 1) * PAGE_SIZE)
        kh = pltpu.einshape("shd->hsd", k_refs[s][...])
        vh = pltpu.einshape("shd->hsd", v_refs[s][...])
        for h in range(SB_HEADS):
            cols = slice(h * SB_HEAD_DIM, (h + 1) * SB_HEAD_DIM)
            kb_ref[rows, cols] = kh[h].astype(BF16)
            vb_ref[rows, cols] = vh[h].astype(BF16)
    n = pp * PAGE_SIZE
    base = (ng - 1 - g) * n
    block(kb_ref[...], vb_ref[...], base + lax.broadcasted_iota(jnp.int32, (n, 1), 0))

    @pl.when(g == ng - 1)
    def _():
        for h in range(SB_HEADS):
            cols = slice(h * SB_HEAD_DIM, (h + 1) * SB_HEAD_DIM)
            o_ref[:, cols] = acc_ref[h * tn:(h + 1) * tn, cols]


def _sb_sample(q, k_new, v_new, cache_k, cache_v, page_table, bias, *, layer, pp):
    nb, n_pages = page_table.shape
    tn = q.shape[0] // nb
    width = SB_HEADS * SB_HEAD_DIM
    hq = SB_HEADS * tn
    ng = n_pages // pp
    assert n_pages % pp == 0
    bias_lane = jnp.repeat(bias, tn).reshape(1, hq)
    row = lambda: pl.BlockSpec((tn, width), lambda b, g, pt: (b, 0))

    def page(s):
        return pl.BlockSpec((None, None, PAGE_SIZE, SB_HEADS, SB_HEAD_DIM),
                            lambda b, g, pt: (layer, pt[b, (ng - 1 - g) * pp + s], 0, 0, 0))

    grid_spec = pltpu.PrefetchScalarGridSpec(
        num_scalar_prefetch=1,
        grid=(nb, ng),
        in_specs=[row(), row(), row(), pl.BlockSpec((1, hq), lambda b, g, pt: (0, 0))]
        + [page(s) for s in range(pp)] + [page(s) for s in range(pp)],
        out_specs=row(),
        scratch_shapes=[
            pltpu.VMEM((width, hq), BF16),
            pltpu.VMEM((pp * PAGE_SIZE, width), BF16),
            pltpu.VMEM((pp * PAGE_SIZE, width), BF16),
            pltpu.VMEM((hq, width), F32),
            pltpu.VMEM((1, hq), F32),
        ],
    )
    return pl.pallas_call(
        functools.partial(_sb_sample_kernel, pp=pp, tn=tn, past_len=n_pages * PAGE_SIZE),
        grid_spec=grid_spec,
        out_shape=jax.ShapeDtypeStruct((nb * tn, width), F32),
        compiler_params=_cparams("parallel", "arbitrary"),
    )(page_table, q, k_new, v_new, bias_lane, *([cache_k] * pp), *([cache_v] * pp))


MM = dict(tm=2064, tn=1024, tk=1024)
ROWS_P = 192


def kernel(x_prompt, x_sample, cache_k, cache_v, state_gla, state_conv, page_table, meta_tokens,
           norm_g, w_in_even, w_gate_up, b_gate, gla_norm_g, conv_w, conv_b, conv_ln_g, conv_ln_b,
           w_out_even, w_qkv, sb_bias, w_out_odd, w_up, w_down):
    nb_p, seq, d = x_prompt.shape
    nb_s, seq_s, _ = x_sample.shape
    t_p = seq + N_META
    rows_s = nb_s * seq_s
    depth = norm_g.shape[0]

    meta = jnp.broadcast_to(meta_tokens[None], (nb_p, N_META, d))
    xp = jnp.concatenate([meta, x_prompt], axis=1).reshape(nb_p * t_p, d)
    xs = x_sample.reshape(rows_s, d)

    hp = _norm(xp, norm_g[0, 0], tr=ROWS_P)
    hs = _norm(xs, norm_g[0, 0], tr=rows_s)

    zero_gla = jnp.zeros((nb_p, GLA_HEADS, GLA_DK, GLA_DV), F32)
    zero_conv = jnp.zeros((nb_p, CONV_W - 1, CONV_CH), F32)
    glu_col0 = 2 * GLA_KEY + 2 * GLA_VAL + GLA_RANK

    pk, pv, sk, sv, pg, sg, pc, sc = [], [], [], [], [], [], [], []
    for layer in range(depth):
        i = layer // 2
        if layer % 2 == 0:
            proj_p, proj_s = _matmul(hp, hs, w_in_even, i, ncols=glu_col0 - GLA_RANK, **MM)
            glu_p, glu_s = _matmul(hp, hs, w_in_even[i:i + 1, :, glu_col0:], 0, **MM)
            wg_pad = jnp.zeros((LANES, GLA_KEY), F32).at[:GLA_RANK].set(w_gate_up[i])
            b_p = _gate(hp, w_in_even, i, wg_pad, b_gate[i], chunk=GLA_CHUNK, tr=ROWS_P)
            b_s = _gate(hs, w_in_even, i, wg_pad, b_gate[i], chunk=seq_s, tr=rows_s)
            o_p, g_p = _gla(proj_p, b_p, zero_gla, gla_norm_g[i], nb=nb_p, t_len=t_p, chunk=GLA_CHUNK,
                            hb=4, tb=t_p // 3, out_dtype=BF16)
            o_s, g_s = _gla(proj_s, b_s, state_gla[i], gla_norm_g[i], nb=nb_s, t_len=seq_s, chunk=seq_s,
                            hb=4, tb=seq_s, out_dtype=F32)
            c_p, cs_p = _conv(glu_p, zero_conv, conv_w[i], conv_b[i], conv_ln_g[i], conv_ln_b[i],
                              nb=nb_p, t_len=t_p, tb=t_p // 3, rc=16, out_dtype=BF16)
            c_s, cs_s = _conv(glu_s, state_conv[i], conv_w[i], conv_b[i], conv_ln_g[i], conv_ln_b[i],
                              nb=nb_s, t_len=seq_s, tb=seq_s, rc=seq_s, out_dtype=F32)
            mix_p = jnp.concatenate([o_p, c_p], axis=1)
            mix_s = jnp.concatenate([o_s, c_s], axis=1).astype(BF16)
            m_p, m_s = _matmul(mix_p, mix_s, w_out_even, i, **MM)
            pg.append(g_p); sg.append(g_s); pc.append(cs_p); sc.append(cs_s)
        else:
            width = SB_HEADS * SB_HEAD_DIM
            q_p, q_s = _matmul(hp, hs, w_qkv, i, col0=0, ncols=width, dt_p=BF16, **MM)
            k_p, k_s = _matmul(hp, hs, w_qkv, i, col0=width, ncols=width, **MM)
            v_p, v_s = _matmul(hp, hs, w_qkv, i, col0=2 * width, ncols=width, **MM)
            a_p = _sb_prompt(q_p, k_p, v_p, sb_bias[i], nb=nb_p, t_len=t_p, tile=256)
            a_s = _sb_sample(q_s, k_s, v_s, cache_k, cache_v, page_table, sb_bias[i], layer=i, pp=4)
            m_p, m_s = _matmul(a_p, a_s.astype(BF16), w_out_odd, i, **MM)
            pk.append(k_p); pv.append(v_p); sk.append(k_s); sv.append(v_s)

        xp, hp = _resnorm(xp, m_p, norm_g[layer, 1], norm_g[layer, 2], tr=ROWS_P)
        xs, hs = _resnorm(xs, m_s, norm_g[layer, 1], norm_g[layer, 2], tr=rows_s)
        up_p, up_s = _matmul(hp, hs, w_up, layer, act="relu2", dt_p=BF16, dt_s=BF16, **MM)
        f_p, f_s = _matmul(up_p, up_s, w_down, layer, **MM)
        if layer + 1 < depth:
            xp, hp = _resnorm(xp, f_p, norm_g[layer, 3], norm_g[layer + 1, 0], tr=ROWS_P)
            xs, hs = _resnorm(xs, f_s, norm_g[layer, 3], norm_g[layer + 1, 0], tr=rows_s)
        else:
            xp = _res(xp, f_p, norm_g[layer, 3], tr=ROWS_P)
            xs = _res(xs, f_s, norm_g[layer, 3], tr=rows_s)

    kv_p = lambda a: jnp.stack(a).reshape(len(a), nb_p, t_p, SB_HEADS, SB_HEAD_DIM)
    kv_s = lambda a: jnp.stack(a).reshape(len(a), nb_s, seq_s, SB_HEADS, SB_HEAD_DIM)
    y_p = xp.reshape(nb_p, t_p, d)[:, N_META:]
    y_s = xs.reshape(nb_s, seq_s, d)
    return (y_p, y_s, kv_p(pk), kv_p(pv), kv_s(sk), kv_s(sv),
            jnp.stack(pg), jnp.stack(sg), jnp.stack(pc), jnp.stack(sc))
```

```python
import functools

import jax
import jax.numpy as jnp
from jax import lax
from jax.experimental import pallas as pl
from jax.experimental.pallas import tpu as pltpu

F32 = jnp.float32
BF16 = jnp.bfloat16

D_MODEL = 4096
N_META = 16
EPS = 1e-6
GLA_HEADS = 8
GLA_DK = 128
GLA_DV = 256
GLA_KEY = GLA_HEADS * GLA_DK
GLA_VAL = GLA_HEADS * GLA_DV
GLA_RANK = 16
GLA_TAU = 16.0
GLA_CHUNK = 16
CONV_CH = 2048
CONV_W = 31
SB_HEADS = 32
SB_HEAD_DIM = 128
PAGE_SIZE = 128
LANES = 128
SUBLANES = 8
CONV_PAD = 32
NEW_ROWS = 16
VMEM_LIMIT_BYTES = 60 * 1024 * 1024


def _cparams(*sem):
    return pltpu.CompilerParams(dimension_semantics=sem, vmem_limit_bytes=VMEM_LIMIT_BYTES)


def _sigmoid(x):
    return 0.5 * jnp.tanh(0.5 * x) + 0.5


def _softplus(x):
    return jnp.maximum(x, 0.0) + jnp.log(1.0 + jnp.exp(-jnp.abs(x)))


def _mm_kernel(*refs, nk, act, shift):
    if shift:
        ap_ref, as_ref, w_ref, wn_ref, op_ref, os_ref, accp_ref, accs_ref = refs
    else:
        ap_ref, as_ref, w_ref, op_ref, os_ref, accp_ref, accs_ref = refs
    i = pl.program_id(1)
    k = pl.program_id(2)
    tm = ap_ref.shape[0]
    w = w_ref[...]
    if shift:
        wide = jnp.concatenate([w, wn_ref[...]], axis=1)
        w = pltpu.roll(wide, wide.shape[1] - shift, axis=1)[:, :w.shape[1]]
    w = w.astype(BF16)

    def epilogue(x):
        if act == "relu2":
            r = jnp.maximum(x, 0.0)
            return r * r
        return x

    def step(with_sample):
        outs = [(op_ref, accp_ref)] + ([(os_ref, accs_ref)] if with_sample else [])
        direct = [act is None and o.dtype == F32 for o, _ in outs]

        def parts():
            if not with_sample:
                return [jnp.dot(ap_ref[...], w, preferred_element_type=F32)]
            a = jnp.concatenate([ap_ref[...], as_ref[...]], axis=0)
            p = jnp.dot(a, w, preferred_element_type=F32)
            return [p[:tm], p[tm:]]

        def apply(first, last):
            for (o_ref, acc_ref), d, p in zip(outs, direct, parts()):
                if first and last:
                    o_ref[...] = epilogue(p).astype(o_ref.dtype)
                elif d:
                    if first:
                        o_ref[...] = p
                    else:
                        o_ref[...] += p
                elif first:
                    acc_ref[...] = p
                elif last:
                    o_ref[...] = epilogue(acc_ref[...] + p).astype(o_ref.dtype)
                else:
                    acc_ref[...] += p

        if nk == 1:
            apply(True, True)
        elif all(direct):
            pl.when(k == 0)(lambda: apply(True, False))
            pl.when(k > 0)(lambda: apply(False, False))
        else:
            pl.when(k == 0)(lambda: apply(True, False))
            pl.when((k > 0) & (k < nk - 1))(lambda: apply(False, False))
            pl.when(k == nk - 1)(lambda: apply(False, True))

    pl.when(i == 0)(lambda: step(True))
    pl.when(i > 0)(lambda: step(False))


def _matmul(ap, as_, w, layer, *, col0=0, ncols=None, act=None, dt_p=F32, dt_s=F32, tm, tn, tk):
    mp, kdim = ap.shape
    ms = as_.shape[0]
    ncols = w.shape[2] if ncols is None else ncols
    shift = col0 % LANES
    assert mp % tm == 0 and kdim % tk == 0 and ncols % tn == 0 and (col0 - shift) % tn == 0
    nj, ni, nk = ncols // tn, mp // tm, kdim // tk
    jb = (col0 - shift) // tn

    def acc_shape(rows, dt):
        direct = nk == 1 or (act is None and dt == F32)
        return (SUBLANES, LANES) if direct else (rows, tn)

    w_specs = [pl.BlockSpec((None, tk, tn), lambda j, i, k: (layer, k, j + jb))]
    if shift:
        w_specs.append(pl.BlockSpec((None, tk, LANES), lambda j, i, k: (layer, k, (j + jb + 1) * (tn // LANES))))
    return pl.pallas_call(
        functools.partial(_mm_kernel, nk=nk, act=act, shift=shift),
        grid=(nj, ni, nk),
        in_specs=[
            pl.BlockSpec((tm, tk), lambda j, i, k: (i, k)),
            pl.BlockSpec((ms, tk), lambda j, i, k: (0, k)),
        ] + w_specs,
        out_specs=[
            pl.BlockSpec((tm, tn), lambda j, i, k: (i, j)),
            pl.BlockSpec((ms, tn), lambda j, i, k: (0, j)),
        ],
        out_shape=[jax.ShapeDtypeStruct((mp, ncols), dt_p), jax.ShapeDtypeStruct((ms, ncols), dt_s)],
        scratch_shapes=[pltpu.VMEM(acc_shape(tm, dt_p), F32), pltpu.VMEM(acc_shape(ms, dt_s), F32)],
        compiler_params=_cparams("parallel", "arbitrary", "arbitrary"),
    )(ap, as_, *([w] * len(w_specs)))


def _rms(x, g):
    return x * lax.rsqrt(jnp.mean(x * x, axis=-1, keepdims=True) + EPS) * g


def _norm_kernel(x_ref, g_ref, h_ref):
    h_ref[...] = _rms(x_ref[...], g_ref[...]).astype(h_ref.dtype)


def _resnorm_kernel(x_ref, m_ref, g1_ref, g2_ref, xo_ref, h_ref):
    x = x_ref[...] + _rms(m_ref[...], g1_ref[...])
    xo_ref[...] = x
    h_ref[...] = _rms(x, g2_ref[...]).astype(h_ref.dtype)


def _res_kernel(x_ref, m_ref, g1_ref, xo_ref):
    xo_ref[...] = x_ref[...] + _rms(m_ref[...], g1_ref[...])


def _row_spec(tr, d):
    return pl.BlockSpec((tr, d), lambda i: (i, 0))


def _vec_spec(d):
    return pl.BlockSpec((1, d), lambda i: (0, 0))


def _norm(x, g, *, tr):
    r, d = x.shape
    return pl.pallas_call(
        _norm_kernel, grid=(r // tr,),
        in_specs=[_row_spec(tr, d), _vec_spec(d)],
        out_specs=_row_spec(tr, d),
        out_shape=jax.ShapeDtypeStruct((r, d), BF16),
        compiler_params=_cparams("parallel"),
    )(x, g.reshape(1, d))


def _resnorm(x, m, g1, g2, *, tr):
    r, d = x.shape
    return pl.pallas_call(
        _resnorm_kernel, grid=(r // tr,),
        in_specs=[_row_spec(tr, d), _row_spec(tr, d), _vec_spec(d), _vec_spec(d)],
        out_specs=[_row_spec(tr, d), _row_spec(tr, d)],
        out_shape=[jax.ShapeDtypeStruct((r, d), F32), jax.ShapeDtypeStruct((r, d), BF16)],
        compiler_params=_cparams("parallel"),
    )(x, m, g1.reshape(1, d), g2.reshape(1, d))


def _res(x, m, g1, *, tr):
    r, d = x.shape
    return pl.pallas_call(
        _res_kernel, grid=(r // tr,),
        in_specs=[_row_spec(tr, d), _row_spec(tr, d), _vec_spec(d)],
        out_specs=_row_spec(tr, d),
        out_shape=jax.ShapeDtypeStruct((r, d), F32),
        compiler_params=_cparams("parallel"),
    )(x, m, g1.reshape(1, d))


def _gate_kernel(h_ref, w_ref, wg_ref, bg_ref, b_ref, *, chunk):
    g = jnp.dot(h_ref[...], w_ref[...].astype(BF16), preferred_element_type=F32)
    lane = lax.broadcasted_iota(jnp.int32, g.shape, 1)
    g = jnp.where(lane < GLA_RANK, g, 0.0)
    pre = jnp.dot(g.astype(BF16), wg_ref[...].astype(BF16), preferred_element_type=F32) + bg_ref[...]
    log_a = (jnp.minimum(pre, 0.0) - jnp.log1p(jnp.exp(-jnp.abs(pre)))) * (1.0 / GLA_TAU)
    tr = log_a.shape[0]
    shift = chunk.bit_length() - 1
    r = lax.broadcasted_iota(jnp.int32, (tr, tr), 0)
    c = lax.broadcasted_iota(jnp.int32, (tr, tr), 1)
    same = jnp.right_shift(r, shift) == jnp.right_shift(c, shift)
    tri = jnp.where(same & (c <= r), 1.0, 0.0).astype(F32)
    b_ref[...] = jnp.dot(tri, log_a, preferred_element_type=F32, precision=lax.Precision.HIGHEST)


def _gate(h, w_in, layer, wg_pad, b_gate, *, chunk, tr):
    r, d = h.shape
    gate_block = (2 * GLA_KEY + 2 * GLA_VAL) // LANES
    return pl.pallas_call(
        functools.partial(_gate_kernel, chunk=chunk), grid=(r // tr,),
        in_specs=[
            _row_spec(tr, d),
            pl.BlockSpec((None, d, LANES), lambda i: (layer, 0, gate_block)),
            pl.BlockSpec((LANES, GLA_KEY), lambda i: (0, 0)),
            _vec_spec(GLA_KEY),
        ],
        out_specs=_row_spec(tr, GLA_KEY),
        out_shape=jax.ShapeDtypeStruct((r, GLA_KEY), F32),
        compiler_params=_cparams("parallel"),
    )(h, w_in, wg_pad, b_gate.reshape(1, GLA_KEY))


def _gla_kernel(q_ref, k_ref, v_ref, r_ref, b_ref, s0_ref, g_ref, o_ref, sn_ref, st_ref, *, chunk, hb, nt):
    t = pl.program_id(2)
    nc = q_ref.shape[0] // chunk

    @pl.when(t == 0)
    def _():
        for hh in range(hb):
            st_ref[hh] = s0_ref[0, hh].T

    row = lax.broadcasted_iota(jnp.int32, (chunk, chunk, GLA_DK), 0)
    col = lax.broadcasted_iota(jnp.int32, (chunk, chunk, GLA_DK), 1)
    causal = col <= row
    gam = g_ref[...]
    nt_dims = (((1,), (1,)), ((), ()))
    tn_dims = (((0,), (0,)), ((), ()))

    def body(c, carry):
        r0 = pl.multiple_of(c * chunk, chunk)
        rows = pl.ds(r0, chunk)
        for hh in range(hb):
            kcols = slice(hh * GLA_DK, (hh + 1) * GLA_DK)
            vcols = slice(hh * GLA_DV, (hh + 1) * GLA_DV)
            q = q_ref[rows, kcols] * (GLA_DK ** -0.5)
            k = k_ref[rows, kcols]
            v = v_ref[rows, vcols]
            b = b_ref[rows, kcols]
            st = st_ref[hh]
            o_inter = lax.dot_general((q * jnp.exp(b)).astype(BF16), st.astype(BF16), nt_dims,
                                      preferred_element_type=F32)
            rel = jnp.where(causal, b[:, None, :] - b[None, :, :], -jnp.inf)
            att = jnp.sum(q[:, None, :] * jnp.exp(rel) * k[None, :, :], axis=-1)
            o_intra = jnp.dot(att.astype(BF16), v.astype(BF16), preferred_element_type=F32)
            b_last = b[chunk - 1:chunk, :]
            kd = k * jnp.exp(b_last - b)
            upd = lax.dot_general(v.astype(BF16), kd.astype(BF16), tn_dims, preferred_element_type=F32)
            st_ref[hh] = jnp.exp(b_last) * st + upd
            o = o_inter + o_intra
            y = o * lax.rsqrt(jnp.mean(o * o, axis=-1, keepdims=True) + EPS) * gam
            rr = r_ref[rows, vcols]
            o_ref[rows, vcols] = (y * (rr * _sigmoid(rr))).astype(o_ref.dtype)
        return carry

    lax.fori_loop(0, nc, body, 0)

    @pl.when(t == nt - 1)
    def _():
        for hh in range(hb):
            sn_ref[0, hh] = st_ref[hh].T


def _gla(proj, bcum, s0, gla_g, *, nb, t_len, chunk, hb, tb, out_dtype):
    nt = t_len // tb
    kw, vw = hb * GLA_DK, hb * GLA_DV
    k_off, v_off, r_off = GLA_KEY // kw, (2 * GLA_KEY) // vw, (2 * GLA_KEY + GLA_VAL) // vw
    rows = lambda b, h, t: b * nt + t
    return pl.pallas_call(
        functools.partial(_gla_kernel, chunk=chunk, hb=hb, nt=nt),
        grid=(nb, GLA_HEADS // hb, nt),
        in_specs=[
            pl.BlockSpec((tb, kw), lambda b, h, t: (rows(b, h, t), h)),
            pl.BlockSpec((tb, kw), lambda b, h, t: (rows(b, h, t), k_off + h)),
            pl.BlockSpec((tb, vw), lambda b, h, t: (rows(b, h, t), v_off + h)),
            pl.BlockSpec((tb, vw), lambda b, h, t: (rows(b, h, t), r_off + h)),
            pl.BlockSpec((tb, kw), lambda b, h, t: (rows(b, h, t), h)),
            pl.BlockSpec((1, hb, GLA_DK, GLA_DV), lambda b, h, t: (b, h, 0, 0)),
            pl.BlockSpec((1, GLA_DV), lambda b, h, t: (0, 0)),
        ],
        out_specs=[
            pl.BlockSpec((tb, vw), lambda b, h, t: (rows(b, h, t), h)),
            pl.BlockSpec((1, hb, GLA_DK, GLA_DV), lambda b, h, t: (b, h, 0, 0)),
        ],
        out_shape=[
            jax.ShapeDtypeStruct((nb * t_len, GLA_VAL), out_dtype),
            jax.ShapeDtypeStruct((nb, GLA_HEADS, GLA_DK, GLA_DV), F32),
        ],
        scratch_shapes=[pltpu.VMEM((hb, GLA_DV, GLA_DK), F32)],
        compiler_params=_cparams("parallel", "parallel", "arbitrary"),
    )(proj, proj, proj, proj, bcum, s0, gla_g.reshape(1, GLA_DV))


def _conv_kernel(a_ref, g_ref, buf_ref, w_ref, cb_ref, lg_ref, lb_ref, c_ref, st_ref, u_ref, *, rc, cc, nt):
    t = pl.program_id(1)
    tb = a_ref.shape[0]
    hist = CONV_W - 1

    @pl.when(t == 0)
    def _():
        u_ref[0:CONV_PAD, :] = jnp.zeros((CONV_PAD, CONV_CH), F32)
        u_ref[CONV_PAD - hist:CONV_PAD, :] = buf_ref[0]

    @pl.when(t > 0)
    def _():
        u_ref[0:CONV_PAD, :] = u_ref[tb:tb + CONV_PAD, :]

    u_ref[CONV_PAD:CONV_PAD + tb, :] = a_ref[...] * _sigmoid(g_ref[...])

    def body(i, carry):
        r0 = pl.multiple_of(i * rc, rc)
        pieces = []
        for c0 in range(0, CONV_CH, cc):
            win = u_ref[pl.ds(r0, rc + CONV_PAD), c0:c0 + cc]
            wt = lambda tap: w_ref[tap:tap + 1, c0:c0 + cc]
            acc = win[CONV_PAD:CONV_PAD + rc, :] * wt(CONV_W - 1)
            for b in range(SUBLANES):
                q = None
                for a in range(CONV_PAD // SUBLANES):
                    tap = SUBLANES * a + b - (CONV_PAD - hist)
                    if tap < 0:
                        continue
                    term = win[SUBLANES * a:SUBLANES * a + rc + SUBLANES, :] * wt(tap)
                    q = term if q is None else q + term
                acc = acc + q[b:b + rc, :]
            pieces.append(acc + cb_ref[:, c0:c0 + cc])
        x = jnp.concatenate(pieces, axis=1)
        mu = jnp.mean(x, axis=-1, keepdims=True)
        xc = x - mu
        var = jnp.mean(xc * xc, axis=-1, keepdims=True)
        y = xc * lax.rsqrt(var + EPS) * lg_ref[...] + lb_ref[...]
        c_ref[pl.ds(r0, rc), :] = (y * _sigmoid(y)).astype(c_ref.dtype)
        return carry

    lax.fori_loop(0, tb // rc, body, 0)

    @pl.when(t == nt - 1)
    def _():
        st_ref[0] = u_ref[tb + CONV_PAD - hist:tb + CONV_PAD, :]


def _conv(glu, buf, conv_w, conv_b, ln_g, ln_b, *, nb, t_len, tb, rc, out_dtype):
    nt = t_len // tb
    vec = lambda: pl.BlockSpec((1, CONV_CH), lambda b, t: (0, 0))
    return pl.pallas_call(
        functools.partial(_conv_kernel, rc=rc, cc=4 * LANES, nt=nt),
        grid=(nb, nt),
        in_specs=[
            pl.BlockSpec((tb, CONV_CH), lambda b, t: (b * nt + t, 0)),
            pl.BlockSpec((tb, CONV_CH), lambda b, t: (b * nt + t, 1)),
            pl.BlockSpec((1, CONV_W - 1, CONV_CH), lambda b, t: (b, 0, 0)),
            pl.BlockSpec((CONV_W, CONV_CH), lambda b, t: (0, 0)),
            vec(), vec(), vec(),
        ],
        out_specs=[
            pl.BlockSpec((tb, CONV_CH), lambda b, t: (b * nt + t, 0)),
            pl.BlockSpec((1, CONV_W - 1, CONV_CH), lambda b, t: (b, 0, 0)),
        ],
        out_shape=[
            jax.ShapeDtypeStruct((nb * t_len, CONV_CH), out_dtype),
            jax.ShapeDtypeStruct((nb, CONV_W - 1, CONV_CH), F32),
        ],
        scratch_shapes=[pltpu.VMEM((tb + CONV_PAD, CONV_CH), F32)],
        compiler_params=_cparams("parallel", "arbitrary"),
    )(glu, glu, buf, conv_w, conv_b.reshape(1, CONV_CH), ln_g.reshape(1, CONV_CH), ln_b.reshape(1, CONV_CH))


def _sb_prompt_kernel(bias_ref, q_ref, k_ref, v_ref, o_ref, kp_ref, vp_ref, *, t_len, tile):
    bias = bias_ref[pl.program_id(1)]
    tp = kp_ref.shape[0]
    kp_ref[0:t_len, :] = k_ref[...].astype(BF16)
    vp_ref[0:t_len, :] = v_ref[...].astype(BF16)
    if tp > t_len:
        kp_ref[t_len:tp, :] = jnp.zeros((tp - t_len, SB_HEAD_DIM), BF16)
        vp_ref[t_len:tp, :] = jnp.zeros((tp - t_len, SB_HEAD_DIM), BF16)
    r = lax.broadcasted_iota(jnp.int32, (2 * tile, tile), 0)
    c = lax.broadcasted_iota(jnp.int32, (2 * tile, tile), 1)
    newer2 = jnp.where(jnp.bitwise_and(r, tile - 1) > c, 1.0, 0.0).astype(BF16)
    for q0 in range(0, t_len, tile):
        q1 = min(q0 + tile, t_len)
        o_ref[q0:q1, :] = _sb_rows(q_ref[q0:q1, :], kp_ref, vp_ref, q0 // tile + 1, tile, bias, newer2)


def _sb_rows(q, kp_ref, vp_ref, n, tile, bias, newer2):
    tq = q.shape[0]
    nt_dims = (((1,), (1,)), ((), ()))
    blk = lambda a, j: a[:, j * tile:(j + 1) * tile]
    z = lax.dot_general(q, kp_ref[0:n * tile, :], nt_dims, preferred_element_type=F32)
    z = z * (SB_HEAD_DIM ** -0.5) + bias
    mask = lax.broadcasted_iota(jnp.int32, (tq, tile), 1) < lax.broadcasted_iota(jnp.int32, (tq, tile), 0)
    sp_all = _softplus(z)
    sp = [blk(sp_all, j) for j in range(n)]
    sp[-1] = jnp.where(mask, sp[-1], 0.0)
    hi = [s.astype(BF16) for s in sp]
    lo = [(s - h.astype(F32)).astype(BF16) for s, h in zip(sp, hi)]
    hilo = jnp.concatenate([jnp.concatenate([h, l], axis=1) for h, l in zip(hi, lo)], axis=0)
    cs = jnp.dot(hilo, newer2, preferred_element_type=F32)
    c = [None] * n
    run = jnp.zeros((tq, 1), F32)
    for j in range(n - 1, -1, -1):
        c[j] = run
        run = run + jnp.sum(sp[j], axis=1, keepdims=True)
    w = []
    for j in range(n):
        wj = jnp.exp(blk(z, j) - sp[j] - cs[j * tq:(j + 1) * tq] - c[j])
        if j == n - 1:
            wj = jnp.where(mask, wj, 0.0)
        w.append(wj.astype(BF16))
    out = jnp.dot(jnp.concatenate(w, axis=1), vp_ref[0:n * tile, :], preferred_element_type=F32)
    return out.astype(BF16)


def _sb_prompt(q, k, v, bias, *, nb, t_len, tile):
    tp = -(-t_len // tile) * tile
    spec = lambda: pl.BlockSpec((t_len, SB_HEAD_DIM), lambda b, h: (b, h))
    return pl.pallas_call(
        functools.partial(_sb_prompt_kernel, t_len=t_len, tile=tile),
        grid=(nb, SB_HEADS),
        in_specs=[pl.BlockSpec(memory_space=pltpu.SMEM), spec(), spec(), spec()],
        out_specs=spec(),
        out_shape=jax.ShapeDtypeStruct((nb * t_len, SB_HEADS * SB_HEAD_DIM), BF16),
        scratch_shapes=[pltpu.VMEM((tp, SB_HEAD_DIM), BF16), pltpu.VMEM((tp, SB_HEAD_DIM), BF16)],
        compiler_params=_cparams("parallel", "parallel"),
    )(bias, q, k, v)


def _sb_sample_kernel(pt_ref, q_ref, kn_ref, vn_ref, bias_ref, *rest, pp, tn, past_len):
    k_refs, v_refs = rest[:pp], rest[pp:2 * pp]
    o_ref, qbd_ref, kb_ref, vb_ref, acc_ref, c_ref = rest[2 * pp:]
    g = pl.program_id(1)
    ng = pl.num_programs(1)
    hq = SB_HEADS * tn
    width = SB_HEADS * SB_HEAD_DIM
    scale = SB_HEAD_DIM ** -0.5
    tn_dims = (((0,), (0,)), ((), ()))
    lane_t = jnp.bitwise_and(lax.broadcasted_iota(jnp.int32, (1, hq), 1), tn - 1)
    bias = bias_ref[...]

    def block(kb, vb, key_pos):
        n = kb.shape[0]
        z = jnp.dot(kb, qbd_ref[...], preferred_element_type=F32) * scale + bias
        mask = key_pos < (past_len + lane_t)
        sp = jnp.where(mask, _softplus(z), 0.0)
        hi = sp.astype(BF16)
        lo = (sp - hi.astype(F32)).astype(BF16)
        r = lax.broadcasted_iota(jnp.int32, (n, n), 0)
        c = lax.broadcasted_iota(jnp.int32, (n, n), 1)
        newer = jnp.where(c > r, 1.0, 0.0).astype(BF16)
        cs = jnp.dot(newer, hi, preferred_element_type=F32) + jnp.dot(newer, lo, preferred_element_type=F32)
        w = jnp.where(mask, jnp.exp(z - sp - cs - c_ref[...]), 0.0)
        acc_ref[...] += lax.dot_general(w.astype(BF16), vb, tn_dims, preferred_element_type=F32)
        c_ref[...] += jnp.sum(sp, axis=0, keepdims=True)

    @pl.when(g == 0)
    def _():
        qt = jnp.concatenate([q_ref[...]] * SB_HEADS, axis=0)
        rh = jnp.right_shift(lax.broadcasted_iota(jnp.int32, (hq, width), 0), tn.bit_length() - 1)
        ch = jnp.right_shift(lax.broadcasted_iota(jnp.int32, (hq, width), 1), SB_HEAD_DIM.bit_length() - 1)
        qbd_ref[...] = jnp.where(rh == ch, qt, 0.0).T.astype(BF16)
        acc_ref[...] = jnp.zeros_like(acc_ref)
        c_ref[...] = jnp.zeros_like(c_ref)
        pad = jnp.zeros((NEW_ROWS - tn, width), F32)
        pos = past_len + lax.broadcasted_iota(jnp.int32, (NEW_ROWS, 1), 0)
        block(jnp.concatenate([kn_ref[...], pad], axis=0).astype(BF16),
              jnp.concatenate([vn_ref[...], pad], axis=0).astype(BF16), pos)

    for s in range(pp):
        rows = slice(s * PAGE_SIZE, (s + 1) * PAGE_SIZE)
        kh = pltpu.einshape("shd->hsd", k_refs[s][...])
        vh = pltpu.einshape("shd->hsd", v_refs[s][...])
        for h in range(SB_HEADS):
            cols = slice(h * SB_HEAD_DIM, (h + 1) * SB_HEAD_DIM)
            kb_ref[rows, cols] = kh[h].astype(BF16)
            vb_ref[rows, cols] = vh[h].astype(BF16)
    n = pp * PAGE_SIZE
    base = (ng - 1 - g) * n
    block(kb_ref[...], vb_ref[...], base + lax.broadcasted_iota(jnp.int32, (n, 1), 0))

    @pl.when(g == ng - 1)
    def _():
        for h in range(SB_HEADS):
            cols = slice(h * SB_HEAD_DIM, (h + 1) * SB_HEAD_DIM)
            o_ref[:, cols] = acc_ref[h * tn:(h + 1) * tn, cols]


def _sb_sample(q, k_new, v_new, cache_k, cache_v, page_table, bias, *, layer, pp):
    nb, n_pages = page_table.shape
    tn = q.shape[0] // nb
    width = SB_HEADS * SB_HEAD_DIM
    hq = SB_HEADS * tn
    ng = n_pages // pp
    assert n_pages % pp == 0
    bias_lane = jnp.repeat(bias, tn).reshape(1, hq)
    row = lambda: pl.BlockSpec((tn, width), lambda b, g, pt: (b, 0))

    def page(s):
        return pl.BlockSpec((None, None, PAGE_SIZE, SB_HEADS, SB_HEAD_DIM),
                            lambda b, g, pt: (layer, pt[b, (ng - 1 - g) * pp + s], 0, 0, 0))

    grid_spec = pltpu.PrefetchScalarGridSpec(
        num_scalar_prefetch=1,
        grid=(nb, ng),
        in_specs=[row(), row(), row(), pl.BlockSpec((1, hq), lambda b, g, pt: (0, 0))]
        + [page(s) for s in range(pp)] + [page(s) for s in range(pp)],
        out_specs=row(),
        scratch_shapes=[
            pltpu.VMEM((width, hq), BF16),
            pltpu.VMEM((pp * PAGE_SIZE, width), BF16),
            pltpu.VMEM((pp * PAGE_SIZE, width), BF16),
            pltpu.VMEM((hq, width), F32),
            pltpu.VMEM((1, hq), F32),
        ],
    )
    return pl.pallas_call(
        functools.partial(_sb_sample_kernel, pp=pp, tn=tn, past_len=n_pages * PAGE_SIZE),
        grid_spec=grid_spec,
        out_shape=jax.ShapeDtypeStruct((nb * tn, width), F32),
        compiler_params=_cparams("parallel", "arbitrary"),
    )(page_table, q, k_new, v_new, bias_lane, *([cache_k] * pp), *([cache_v] * pp))


MM = dict(tm=2064, tn=1024, tk=1024)
ROWS_P = 192
CONV_ROWS = 48


def kernel(x_prompt, x_sample, cache_k, cache_v, state_gla, state_conv, page_table, meta_tokens,
           norm_g, w_in_even, w_gate_up, b_gate, gla_norm_g, conv_w, conv_b, conv_ln_g, conv_ln_b,
           w_out_even, w_qkv, sb_bias, w_out_odd, w_up, w_down):
    nb_p, seq, d = x_prompt.shape
    nb_s, seq_s, _ = x_sample.shape
    t_p = seq + N_META
    rows_s = nb_s * seq_s
    depth = norm_g.shape[0]

    meta = jnp.broadcast_to(meta_tokens[None], (nb_p, N_META, d))
    xp = jnp.concatenate([meta, x_prompt], axis=1).reshape(nb_p * t_p, d)
    xs = x_sample.reshape(rows_s, d)

    hp = _norm(xp, norm_g[0, 0], tr=ROWS_P)
    hs = _norm(xs, norm_g[0, 0], tr=rows_s)

    zero_gla = jnp.zeros((nb_p, GLA_HEADS, GLA_DK, GLA_DV), F32)
    zero_conv = jnp.zeros((nb_p, CONV_W - 1, CONV_CH), F32)
    glu_col0 = 2 * GLA_KEY + 2 * GLA_VAL + GLA_RANK

    pk, pv, sk, sv, pg, sg, pc, sc = [], [], [], [], [], [], [], []
    for layer in range(depth):
        i = layer // 2
        if layer % 2 == 0:
            proj_p, proj_s = _matmul(hp, hs, w_in_even, i, ncols=glu_col0 - GLA_RANK, **MM)
            glu_p, glu_s = _matmul(hp, hs, w_in_even, i, col0=glu_col0, ncols=2 * CONV_CH, **MM)
            wg_pad = jnp.zeros((LANES, GLA_KEY), F32).at[:GLA_RANK].set(w_gate_up[i])
            b_p = _gate(hp, w_in_even, i, wg_pad, b_gate[i], chunk=GLA_CHUNK, tr=ROWS_P)
            b_s = _gate(hs, w_in_even, i, wg_pad, b_gate[i], chunk=seq_s, tr=rows_s)
            o_p, g_p = _gla(proj_p, b_p, zero_gla, gla_norm_g[i], nb=nb_p, t_len=t_p, chunk=GLA_CHUNK,
                            hb=GLA_HEADS, tb=t_p // 3, out_dtype=BF16)
            o_s, g_s = _gla(proj_s, b_s, state_gla[i], gla_norm_g[i], nb=nb_s, t_len=seq_s, chunk=seq_s,
                            hb=4, tb=seq_s, out_dtype=F32)
            c_p, cs_p = _conv(glu_p, zero_conv, conv_w[i], conv_b[i], conv_ln_g[i], conv_ln_b[i],
                              nb=nb_p, t_len=t_p, tb=CONV_ROWS, rc=CONV_ROWS, out_dtype=BF16)
            c_s, cs_s = _conv(glu_s, state_conv[i], conv_w[i], conv_b[i], conv_ln_g[i], conv_ln_b[i],
                              nb=nb_s, t_len=seq_s, tb=seq_s, rc=seq_s, out_dtype=F32)
            mix_p = jnp.concatenate([o_p, c_p], axis=1)
            mix_s = jnp.concatenate([o_s, c_s], axis=1).astype(BF16)
            m_p, m_s = _matmul(mix_p, mix_s, w_out_even, i, **MM)
            pg.append(g_p); sg.append(g_s); pc.append(cs_p); sc.append(cs_s)
        else:
            width = SB_HEADS * SB_HEAD_DIM
            q_p, q_s = _matmul(hp, hs, w_qkv, i, col0=0, ncols=width, dt_p=BF16, **MM)
            k_p, k_s = _matmul(hp, hs, w_qkv, i, col0=width, ncols=width, **MM)
            v_p, v_s = _matmul(hp, hs, w_qkv, i, col0=2 * width, ncols=width, **MM)
            a_p = _sb_prompt(q_p, k_p, v_p, sb_bias[i], nb=nb_p, t_len=t_p, tile=256)
            a_s = _sb_sample(q_s, k_s, v_s, cache_k, cache_v, page_table, sb_bias[i], layer=i, pp=4)
            m_p, m_s = _matmul(a_p, a_s.astype(BF16), w_out_odd, i, **MM)
            pk.append(k_p); pv.append(v_p); sk.append(k_s); sv.append(v_s)

        xp, hp = _resnorm(xp, m_p, norm_g[layer, 1], norm_g[layer, 2], tr=ROWS_P)
        xs, hs = _resnorm(xs, m_s, norm_g[layer, 1], norm_g[layer, 2], tr=rows_s)
        up_p, up_s = _matmul(hp, hs, w_up, layer, act="relu2", dt_p=BF16, dt_s=BF16, **MM)
        f_p, f_s = _matmul(up_p, up_s, w_down, layer, **MM)
        if layer + 1 < depth:
            xp, hp = _resnorm(xp, f_p, norm_g[layer, 3], norm_g[layer + 1, 0], tr=ROWS_P)
            xs, hs = _resnorm(xs, f_s, norm_g[layer, 3], norm_g[layer + 1, 0], tr=rows_s)
        else:
            xp = _res(xp, f_p, norm_g[layer, 3], tr=ROWS_P)
            xs = _res(xs, f_s, norm_g[layer, 3], tr=rows_s)

    kv_p = lambda a: jnp.stack(a).reshape(len(a), nb_p, t_p, SB_HEADS, SB_HEAD_DIM)
    kv_s = lambda a: jnp.stack(a).reshape(len(a), nb_s, seq_s, SB_HEADS, SB_HEAD_DIM)
    y_p = xp.reshape(nb_p, t_p, d)[:, N_META:]
    y_s = xs.reshape(nb_s, seq_s, d)
    return (y_p, y_s, kv_p(pk), kv_p(pv), kv_s(sk), kv_s(sv),
            jnp.stack(pg), jnp.stack(sg), jnp.stack(pc), jnp.stack(sc))
```
